```python
import jax, jax.numpy as jnp
from jax import lax
import numpy as np

D_MODEL = 2048
BATCH = 2
SEQ = 16384
DEPTH = 4
DEC_BATCH = 4
DEC_SEQ = 4096
PAST_LEN = 128

N_MIXERS = 4
EPS = 1e-6
D_FF = ((8 * D_MODEL // 3 + 255) // 256) * 256
POOL_WINDOWS = (2, 4, 8, 16)
N_POOL_GROUPS = len(POOL_WINDOWS)
POOL_GROUP_DIM = D_MODEL // N_POOL_GROUPS
SC_WIDTH = 3
CF_WIDTH = 31
SG_CHUNK = 128
SG_HEADS = 8
SG_HEAD_DIM = D_MODEL // SG_HEADS
N_LAYERS_A = (DEPTH + 3) // 4
N_LAYERS_B = (DEPTH + 2) // 4
N_LAYERS_C = (DEPTH + 1) // 4
N_LAYERS_D = DEPTH // 4

kernel_name = "hybrid_pool_conv_conformer_sgu_encoder"


def rms_norm(x, g):
    xf = x.astype(jnp.float32)
    y = xf * lax.rsqrt(jnp.mean(xf * xf, axis=-1, keepdims=True) + EPS)
    return (y * g.astype(jnp.float32)).astype(x.dtype)


def layer_norm(x, g, b):
    xf = x.astype(jnp.float32)
    mu = jnp.mean(xf, axis=-1, keepdims=True)
    xc = xf - mu
    y = xc * lax.rsqrt(jnp.mean(xc * xc, axis=-1, keepdims=True) + EPS)
    return (y * g.astype(jnp.float32) + b.astype(jnp.float32)).astype(x.dtype)


def depthwise_conv(x, w):
    k = w.shape[0]
    return lax.conv_general_dilated(
        x, w[:, None, :], window_strides=(1,), padding=((k // 2, k // 2),),
        dimension_numbers=("NWC", "WIO", "NWC"), feature_group_count=x.shape[-1])


def pool_mixer(x, w_in, w_grp, scale, w_out):
    b, s, _ = x.shape
    h = (x @ w_in).reshape(b, s, N_POOL_GROUPS, POOL_GROUP_DIM)
    csum = jnp.cumsum(h.astype(jnp.float32), axis=1)
    csum = jnp.concatenate([jnp.zeros_like(csum[:, :1]), csum], axis=1)
    t = jnp.arange(s, dtype=jnp.int32)
    pooled = []
    for g, w in enumerate(POOL_WINDOWS):
        half = w // 2
        cg = jnp.pad(csum[:, :, g], ((0, 0), (half, half), (0, 0)), mode="edge")
        total = cg[:, w:w + s] - cg[:, :s]
        count = (jnp.minimum(t + half, s) - jnp.maximum(t - half, 0)).astype(jnp.float32)
        pooled.append(total / count[None, :, None])
    pooled = jnp.stack(pooled, axis=2).astype(x.dtype) - h
    mixed = jnp.einsum("bsgc,gcd->bsgd", pooled, w_grp).reshape(b, s, D_MODEL) * scale
    return mixed @ w_out


def short_conv_mixer(x, w_in, conv_w, w_out):
    gb, gc, h = jnp.split(x @ w_in, 3, axis=-1)
    return (gb * depthwise_conv(gc * h, conv_w)) @ w_out


def conformer_mixer(x, w_in, dw_w, dw_b, ln_g, ln_b, w_out):
    a, gate = jnp.split(x @ w_in, 2, axis=-1)
    h = a * jax.nn.sigmoid(gate)
    h = depthwise_conv(h, dw_w) + dw_b
    h = jax.nn.silu(layer_norm(h, ln_g, ln_b))
    return h @ w_out


def spatial_gating_mixer(x, w_in, ln_g, ln_b, w_s, b_s, w_out):
    b, s, _ = x.shape
    u, v = jnp.split(jax.nn.gelu(x @ w_in, approximate=False), 2, axis=-1)
    v = layer_norm(v, ln_g, ln_b).reshape(b, s // SG_CHUNK, SG_CHUNK, SG_HEADS, SG_HEAD_DIM)
    v = jnp.einsum("bnphc,hqp->bnqhc", v, w_s) + jnp.transpose(b_s)[:, :, None]
    return (u * v.reshape(b, s, D_MODEL)) @ w_out


def swiglu(x, w_gate, w_up, w_down):
    return (jax.nn.silu(x @ w_gate) * (x @ w_up)) @ w_down


def run_trunk(x, mix_pre_g, mix_post_g, ffn_pre_g, ffn_post_g, ffn_w_gate, ffn_w_up,
              ffn_w_down, pool_w_in, pool_w_grp, pool_scale, pool_w_out, sc_w_in,
              sc_conv_w, sc_w_out, cf_w_in, cf_dw_w, cf_dw_b, cf_ln_g, cf_ln_b,
              cf_w_out, sg_w_in, sg_ln_g, sg_ln_b, sg_w_s, sg_b_s, sg_w_out):
    for i in range(DEPTH):
        kind, j = i % N_MIXERS, i // N_MIXERS
        h = rms_norm(x, mix_pre_g[i])
        if kind == 0:
            h = pool_mixer(h, pool_w_in[j], pool_w_grp[j], pool_scale[j], pool_w_out[j])
        elif kind == 1:
            h = short_conv_mixer(h, sc_w_in[j], sc_conv_w[j], sc_w_out[j])
        elif kind == 2:
            h = conformer_mixer(h, cf_w_in[j], cf_dw_w[j], cf_dw_b[j], cf_ln_g[j],
                                cf_ln_b[j], cf_w_out[j])
        else:
            h = spatial_gating_mixer(h, sg_w_in[j], sg_ln_g[j], sg_ln_b[j], sg_w_s[j],
                                     sg_b_s[j], sg_w_out[j])
        x = x + rms_norm(h, mix_post_g[i])
        h = swiglu(rms_norm(x, ffn_pre_g[i]), ffn_w_gate[i], ffn_w_up[i], ffn_w_down[i])
        x = x + rms_norm(h, ffn_post_g[i])
    return x


def setup_inputs(seed: int = 0) -> dict:
    key = jax.random.key(seed)
    k = jax.random.split(key, 32)
    f32 = jnp.float32

    def dense(kk, shape, fan_in):
        return jax.random.normal(kk, shape, f32) * (fan_in ** -0.5)

    def gain(kk, shape):
        return 1.0 + 0.05 * jax.random.normal(kk, shape, f32)

    def bias(kk, shape):
        return 0.02 * jax.random.normal(kk, shape, f32)

    D = D_MODEL
    return {
        "x_prompt": jax.random.normal(k[0], (BATCH, SEQ, D), f32),
        "x_sample": jax.random.normal(k[1], (DEC_BATCH, DEC_SEQ, D), f32),
        "mix_pre_g": gain(k[2], (DEPTH, D)),
        "mix_post_g": gain(k[3], (DEPTH, D)),
        "ffn_pre_g": gain(k[4], (DEPTH, D)),
        "ffn_post_g": gain(k[5], (DEPTH, D)),
        "ffn_w_gate": dense(k[6], (DEPTH, D, D_FF), D),
        "ffn_w_up": dense(k[7], (DEPTH, D, D_FF), D),
        "ffn_w_down": dense(k[8], (DEPTH, D_FF, D), D_FF),
        "pool_w_in": dense(k[9], (N_LAYERS_A, D, D), D),
        "pool_w_grp": dense(k[10], (N_LAYERS_A, N_POOL_GROUPS, POOL_GROUP_DIM, POOL_GROUP_DIM), POOL_GROUP_DIM),
        "pool_scale": gain(k[11], (N_LAYERS_A, D)),
        "pool_w_out": dense(k[12], (N_LAYERS_A, D, D), D),
        "sc_w_in": dense(k[13], (N_LAYERS_B, D, 3 * D), D),
        "sc_conv_w": dense(k[14], (N_LAYERS_B, SC_WIDTH, D), SC_WIDTH),
        "sc_w_out": dense(k[15], (N_LAYERS_B, D, D), D),
        "cf_w_in": dense(k[16], (N_LAYERS_C, D, 2 * D), D),
        "cf_dw_w": dense(k[17], (N_LAYERS_C, CF_WIDTH, D), CF_WIDTH),
        "cf_dw_b": bias(k[18], (N_LAYERS_C, D)),
        "cf_ln_g": gain(k[19], (N_LAYERS_C, D)),
        "cf_ln_b": bias(k[20], (N_LAYERS_C, D)),
        "cf_w_out": dense(k[21], (N_LAYERS_C, D, D), D),
        "sg_w_in": dense(k[22], (N_LAYERS_D, D, 2 * D), D),
        "sg_ln_g": gain(k[23], (N_LAYERS_D, D)),
        "sg_ln_b": bias(k[24], (N_LAYERS_D, D)),
        "sg_w_s": dense(k[25], (N_LAYERS_D, SG_HEADS, SG_CHUNK, SG_CHUNK), SG_CHUNK),
        "sg_b_s": gain(k[26], (N_LAYERS_D, SG_HEADS, SG_CHUNK)),
        "sg_w_out": dense(k[27], (N_LAYERS_D, D, D), D),
    }


def reference(x_prompt, x_sample, mix_pre_g, mix_post_g, ffn_pre_g, ffn_post_g,
              ffn_w_gate, ffn_w_up, ffn_w_down, pool_w_in, pool_w_grp, pool_scale,
              pool_w_out, sc_w_in, sc_conv_w, sc_w_out, cf_w_in, cf_dw_w, cf_dw_b,
              cf_ln_g, cf_ln_b, cf_w_out, sg_w_in, sg_ln_g, sg_ln_b, sg_w_s, sg_b_s,
              sg_w_out):
    y_prompt = run_trunk(x_prompt, mix_pre_g, mix_post_g, ffn_pre_g, ffn_post_g,
                         ffn_w_gate, ffn_w_up, ffn_w_down, pool_w_in, pool_w_grp,
                         pool_scale, pool_w_out, sc_w_in, sc_conv_w, sc_w_out, cf_w_in,
                         cf_dw_w, cf_dw_b, cf_ln_g, cf_ln_b, cf_w_out, sg_w_in, sg_ln_g,
                         sg_ln_b, sg_w_s, sg_b_s, sg_w_out)
    y_sample = run_trunk(x_sample, mix_pre_g, mix_post_g, ffn_pre_g, ffn_post_g,
                         ffn_w_gate, ffn_w_up, ffn_w_down, pool_w_in, pool_w_grp,
                         pool_scale, pool_w_out, sc_w_in, sc_conv_w, sc_w_out, cf_w_in,
                         cf_dw_w, cf_dw_b, cf_ln_g, cf_ln_b, cf_w_out, sg_w_in, sg_ln_g,
                         sg_ln_b, sg_w_s, sg_b_s, sg_w_out)
    return (y_prompt, y_sample)
```

```python
import functools

import jax
import jax.numpy as jnp
from jax import lax
from jax.experimental import pallas as pl
from jax.experimental.pallas import tpu as pltpu

EPS = 1e-6
POOL_WINDOWS = (2, 4, 8, 16)
SC_WIDTH = 3
CF_WIDTH = 31
SG_CHUNK = 128
SG_HEADS = 8
N_MIXERS = 4

HALO = 16
VMEM_LIMIT_BYTES = 60 * 1024 * 1024

_BF16 = jnp.bfloat16
_F32 = jnp.float32


def _rms_norm(x, g):
    return x * lax.rsqrt(jnp.mean(x * x, axis=-1, keepdims=True) + EPS) * g


def _dot(a, b):
    return jnp.dot(a, b, preferred_element_type=_F32)


def _residual_out(x_ref, acc, post_g_ref, o_ref):
    o_ref[...] = x_ref[...] + _rms_norm(acc, post_g_ref[...])


def _fill_xn_ext(xp_ref, x_ref, xn_ref, g_ref, xe_ref, tiles_per_seq):
    ts = x_ref.shape[0]
    t = pl.program_id(0) % tiles_per_seq
    g = g_ref[...]
    prev = _rms_norm(xp_ref[...], g)
    nxt = _rms_norm(xn_ref[...], g)
    xe_ref[0:HALO, :] = jnp.where(t > 0, prev, 0.0).astype(_BF16)
    xe_ref[HALO:HALO + ts, :] = _rms_norm(x_ref[...], g).astype(_BF16)
    xe_ref[HALO + ts:, :] = jnp.where(t < tiles_per_seq - 1, nxt, 0.0).astype(_BF16)


def _ffn_kernel(x_ref, pre_g_ref, post_g_ref, wg_ref, wu_ref, wd_ref, o_ref, xn_ref, acc_ref):
    j = pl.program_id(1)

    @pl.when(j == 0)
    def _():
        xn_ref[...] = _rms_norm(x_ref[...], pre_g_ref[...]).astype(_BF16)
        acc_ref[...] = jnp.zeros_like(acc_ref)

    xn = xn_ref[...]
    gate = _dot(xn, wg_ref[...])
    up = _dot(xn, wu_ref[...])
    a = (gate * jax.nn.sigmoid(gate) * up).astype(_BF16)
    acc_ref[...] += _dot(a, wd_ref[...])

    @pl.when(j == pl.num_programs(1) - 1)
    def _():
        _residual_out(x_ref, acc_ref[...], post_g_ref, o_ref)


def _ffn(x, pre_g, post_g, wg, wu, wd, *, tm, tf):
    n, d = x.shape
    f = wg.shape[1]
    row = lambda i, j: (i, 0)
    vec = lambda i, j: (0, 0)
    return pl.pallas_call(
        _ffn_kernel,
        grid=(n // tm, f // tf),
        in_specs=[
            pl.BlockSpec((tm, d), row),
            pl.BlockSpec((1, d), vec),
            pl.BlockSpec((1, d), vec),
            pl.BlockSpec((d, tf), lambda i, j: (0, j)),
            pl.BlockSpec((d, tf), lambda i, j: (0, j)),
            pl.BlockSpec((tf, d), lambda i, j: (j, 0)),
        ],
        out_specs=pl.BlockSpec((tm, d), row),
        out_shape=jax.ShapeDtypeStruct((n, d), _F32),
        scratch_shapes=[pltpu.VMEM((tm, d), _BF16), pltpu.VMEM((tm, d), _F32)],
        compiler_params=pltpu.CompilerParams(
            dimension_semantics=("arbitrary", "arbitrary"), vmem_limit_bytes=VMEM_LIMIT_BYTES),
        name="ffn",
    )(x, pre_g, post_g, wg, wu, wd)


def _halo_specs(ts, d, n_rows):
    r = ts // HALO
    last = n_rows // HALO - 1
    return [
        pl.BlockSpec((HALO, d), lambda i, j: (jnp.maximum(i * r - 1, 0), 0)),
        pl.BlockSpec((ts, d), lambda i, j: (i, 0)),
        pl.BlockSpec((HALO, d), lambda i, j: (jnp.minimum((i + 1) * r, last), 0)),
    ]


def _mixer_params():
    return pltpu.CompilerParams(
        dimension_semantics=("arbitrary", "arbitrary"), vmem_limit_bytes=VMEM_LIMIT_BYTES)


def _pool_kernel(xp_ref, x_ref, xn_ref, pre_g_ref, post_g_ref, win_ref, wgrp_ref, scale_ref,
                 wout_ref, o_ref, xe_ref, h_ref, tot_ref, acc_ref, *, seq_len):
    ts = x_ref.shape[0]
    tiles_per_seq = seq_len // ts
    j = pl.program_id(1)

    @pl.when(j == 0)
    def _():
        _fill_xn_ext(xp_ref, x_ref, xn_ref, pre_g_ref, xe_ref, tiles_per_seq)
        acc_ref[...] = jnp.zeros_like(acc_ref)

    h_ref[...] = _dot(xe_ref[...], win_ref[...])

    pos = (pl.program_id(0) % tiles_per_seq) * ts + lax.broadcasted_iota(jnp.int32, (ts, 1), 0)
    for g, w in enumerate(POOL_WINDOWS):
        half = w // 2

        @pl.when(j == g)
        def _(half=half):
            total = h_ref[pl.ds(HALO - half, ts), :]
            for o in range(-half + 1, half):
                total = total + h_ref[pl.ds(HALO + o, ts), :]
            count = (jnp.minimum(pos + half, seq_len) - jnp.maximum(pos - half, 0)).astype(_F32)
            tot_ref[...] = total / count

    pooled = tot_ref[...] - h_ref[pl.ds(HALO, ts), :]
    mixed = _dot(pooled.astype(_BF16), wgrp_ref[...]) * scale_ref[...]
    acc_ref[...] += _dot(mixed.astype(_BF16), wout_ref[...])

    @pl.when(j == pl.num_programs(1) - 1)
    def _():
        _residual_out(x_ref, acc_ref[...], post_g_ref, o_ref)


def _pool_mixer(x, pre_g, post_g, w_in, w_grp, scale, w_out, *, seq_len, ts):
    n, d = x.shape
    ng, gd = w_grp.shape[0], w_grp.shape[1]
    vec = lambda i, j: (0, 0)
    return pl.pallas_call(
        functools.partial(_pool_kernel, seq_len=seq_len),
        grid=(n // ts, ng),
        in_specs=_halo_specs(ts, d, n) + [
            pl.BlockSpec((1, d), vec),
            pl.BlockSpec((1, d), vec),
            pl.BlockSpec((d, gd), lambda i, j: (0, j)),
            pl.BlockSpec((None, gd, gd), lambda i, j: (j, 0, 0)),
            pl.BlockSpec((1, gd), lambda i, j: (0, j)),
            pl.BlockSpec((gd, d), lambda i, j: (j, 0)),
        ],
        out_specs=pl.BlockSpec((ts, d), lambda i, j: (i, 0)),
        out_shape=jax.ShapeDtypeStruct((n, d), _F32),
        scratch_shapes=[
            pltpu.VMEM((ts + 2 * HALO, d), _BF16),
            pltpu.VMEM((ts + 2 * HALO, gd), _F32),
            pltpu.VMEM((ts, gd), _F32),
            pltpu.VMEM((ts, d), _F32),
        ],
        compiler_params=_mixer_params(),
        name="pool_mixer",
    )(x, x, x, pre_g, post_g, w_in, w_grp, scale, w_out)


def _sconv_kernel(xp_ref, x_ref, xn_ref, pre_g_ref, post_g_ref, wb_ref, wc_ref, wh_ref, cw_ref,
                  wout_ref, o_ref, xe_ref, p_ref, acc_ref, *, seq_len):
    ts = x_ref.shape[0]
    j = pl.program_id(1)

    @pl.when(j == 0)
    def _():
        _fill_xn_ext(xp_ref, x_ref, xn_ref, pre_g_ref, xe_ref, seq_len // ts)
        acc_ref[...] = jnp.zeros_like(acc_ref)

    xe = xe_ref[...]
    p_ref[...] = _dot(xe, wc_ref[...]) * _dot(xe, wh_ref[...])
    half = SC_WIDTH // 2
    y = cw_ref[0:1, :] * p_ref[pl.ds(HALO - half, ts), :]
    for k in range(1, SC_WIDTH):
        y = y + cw_ref[k:k + 1, :] * p_ref[pl.ds(HALO - half + k, ts), :]
    gb = _dot(xe_ref[pl.ds(HALO, ts), :], wb_ref[...])
    acc_ref[...] += _dot((gb * y).astype(_BF16), wout_ref[...])

    @pl.when(j == pl.num_programs(1) - 1)
    def _():
        _residual_out(x_ref, acc_ref[...], post_g_ref, o_ref)


def _sconv_mixer(x, pre_g, post_g, w_in, conv_w, w_out, *, seq_len, ts, tc):
    n, d = x.shape
    nc = d // tc
    vec = lambda i, j: (0, 0)
    return pl.pallas_call(
        functools.partial(_sconv_kernel, seq_len=seq_len),
        grid=(n // ts, nc),
        in_specs=_halo_specs(ts, d, n) + [
            pl.BlockSpec((1, d), vec),
            pl.BlockSpec((1, d), vec),
            pl.BlockSpec((d, tc), lambda i, j: (0, j)),
            pl.BlockSpec((d, tc), lambda i, j: (0, nc + j)),
            pl.BlockSpec((d, tc), lambda i, j: (0, 2 * nc + j)),
            pl.BlockSpec((SC_WIDTH, tc), lambda i, j: (0, j)),
            pl.BlockSpec((tc, d), lambda i, j: (j, 0)),
        ],
        out_specs=pl.BlockSpec((ts, d), lambda i, j: (i, 0)),
        out_shape=jax.ShapeDtypeStruct((n, d), _F32),
        scratch_shapes=[
            pltpu.VMEM((ts + 2 * HALO, d), _BF16),
            pltpu.VMEM((ts + 2 * HALO, tc), _F32),
            pltpu.VMEM((ts, d), _F32),
        ],
        compiler_params=_mixer_params(),
        name="sconv_mixer",
    )(x, x, x, pre_g, post_g, w_in, w_in, w_in, conv_w, w_out)


def _layer_norm_stats(y_ref):
    nc, _, tc = y_ref.shape
    inv_d = 1.0 / (nc * tc)
    mu = sum(jnp.sum(y_ref[c], axis=-1, keepdims=True) for c in range(nc)) * inv_d
    var = sum(jnp.sum(jnp.square(y_ref[c] - mu), axis=-1, keepdims=True) for c in range(nc)) * inv_d
    return mu, lax.rsqrt(var + EPS)


def _conformer_kernel(xp_ref, x_ref, xn_ref, pre_g_ref, post_g_ref, wa_ref, wgate_ref, dww_ref,
                      dwb_ref, lng_ref, lnb_ref, wout_ref, o_ref, xe_ref, p_ref, y_ref, *, seq_len):
    ts = x_ref.shape[0]
    nc, _, tc = y_ref.shape
    j = pl.program_id(1)

    @pl.when(j == 0)
    def _():
        _fill_xn_ext(xp_ref, x_ref, xn_ref, pre_g_ref, xe_ref, seq_len // ts)

    xe = xe_ref[...]
    p_ref[...] = _dot(xe, wa_ref[...]) * jax.nn.sigmoid(_dot(xe, wgate_ref[...]))
    half = CF_WIDTH // 2
    y = dww_ref[0:1, :] * p_ref[pl.ds(HALO - half, ts), :]
    for k in range(1, CF_WIDTH):
        y = y + dww_ref[k:k + 1, :] * p_ref[pl.ds(HALO - half + k, ts), :]
    y_ref[j] = y + dwb_ref[...]

    @pl.when(j == nc - 1)
    def _():
        mu, rstd = _layer_norm_stats(y_ref)
        acc = jnp.zeros((ts, wout_ref.shape[1]), _F32)
        for c in range(nc):
            cols = slice(c * tc, (c + 1) * tc)
            z = (y_ref[c] - mu) * rstd * lng_ref[:, cols] + lnb_ref[:, cols]
            z = z * jax.nn.sigmoid(z)
            acc = acc + _dot(z.astype(_BF16), wout_ref[cols, :])
        _residual_out(x_ref, acc, post_g_ref, o_ref)


def _conformer_mixer(x, pre_g, post_g, w_in, dw_w, dw_b, ln_g, ln_b, w_out, *, seq_len, ts, tc):
    n, d = x.shape
    nc = d // tc
    vec = lambda i, j: (0, 0)
    return pl.pallas_call(
        functools.partial(_conformer_kernel, seq_len=seq_len),
        grid=(n // ts, nc),
        in_specs=_halo_specs(ts, d, n) + [
            pl.BlockSpec((1, d), vec),
            pl.BlockSpec((1, d), vec),
            pl.BlockSpec((d, tc), lambda i, j: (0, j)),
            pl.BlockSpec((d, tc), lambda i, j: (0, nc + j)),
            pl.BlockSpec((CF_WIDTH, tc), lambda i, j: (0, j)),
            pl.BlockSpec((1, tc), lambda i, j: (0, j)),
            pl.BlockSpec((1, d), vec),
            pl.BlockSpec((1, d), vec),
            pl.BlockSpec((d, d), vec),
        ],
        out_specs=pl.BlockSpec((ts, d), lambda i, j: (i, 0)),
        out_shape=jax.ShapeDtypeStruct((n, d), _F32),
        scratch_shapes=[
            pltpu.VMEM((ts + 2 * HALO, d), _BF16),
            pltpu.VMEM((ts + 2 * HALO, tc), _F32),
            pltpu.VMEM((nc, ts, tc), _F32),
        ],
        compiler_params=_mixer_params(),
        name="conformer_mixer",
    )(x, x, x, pre_g, post_g, w_in, w_in, dw_w, dw_b, ln_g, ln_b, w_out)


def _gelu(x):
    return 0.5 * x * (1.0 + lax.erf(x * (2.0 ** -0.5)))


def _sgu_kernel(x_ref, pre_g_ref, post_g_ref, wu_ref, wv_ref, lng_ref, lnb_ref, ws_ref, bs_ref,
                wout_ref, o_ref, xn_ref, u_ref, v_ref):
    ts = x_ref.shape[0]
    nc, _, tc = v_ref.shape
    n_heads = ws_ref.shape[0]
    hd = (nc * tc) // n_heads
    j = pl.program_id(1)

    @pl.when(j == 0)
    def _():
        xn_ref[...] = _rms_norm(x_ref[...], pre_g_ref[...]).astype(_BF16)

    xn = xn_ref[...]
    u_ref[j] = _gelu(_dot(xn, wu_ref[...]))
    v_ref[j] = _gelu(_dot(xn, wv_ref[...]))

    @pl.when(j == nc - 1)
    def _():
        mu, rstd = _layer_norm_stats(v_ref)
        acc = jnp.zeros((ts, wout_ref.shape[1]), _F32)
        for c in range(nc):
            cols = slice(c * tc, (c + 1) * tc)
            vn = ((v_ref[c] - mu) * rstd * lng_ref[:, cols] + lnb_ref[:, cols]).astype(_BF16)
            gated = []
            for r in range(ts // SG_CHUNK):
                rows = slice(r * SG_CHUNK, (r + 1) * SG_CHUNK)
                parts = []
                for hh in range(tc // hd):
                    head = c * (tc // hd) + hh
                    s = _dot(ws_ref[head], vn[rows, hh * hd:(hh + 1) * hd])
                    parts.append(s + bs_ref[:, head:head + 1])
                gated.append(jnp.concatenate(parts, axis=1) if len(parts) > 1 else parts[0])
            sv = jnp.concatenate(gated, axis=0) if len(gated) > 1 else gated[0]
            acc = acc + _dot((u_ref[c] * sv).astype(_BF16), wout_ref[cols, :])
        _residual_out(x_ref, acc, post_g_ref, o_ref)


def _sgu_mixer(x, pre_g, post_g, w_in, ln_g, ln_b, w_s, b_s_t, w_out, *, ts, tc):
    n, d = x.shape
    nc = d // tc
    vec = lambda i, j: (0, 0)
    return pl.pallas_call(
        _sgu_kernel,
        grid=(n // ts, nc),
        in_specs=[
            pl.BlockSpec((ts, d), lambda i, j: (i, 0)),
            pl.BlockSpec((1, d), vec),
            pl.BlockSpec((1, d), vec),
            pl.BlockSpec((d, tc), lambda i, j: (0, j)),
            pl.BlockSpec((d, tc), lambda i, j: (0, nc + j)),
            pl.BlockSpec((1, d), vec),
            pl.BlockSpec((1, d), vec),
            pl.BlockSpec(w_s.shape, lambda i, j: (0, 0, 0)),
            pl.BlockSpec(b_s_t.shape, vec),
            pl.BlockSpec((d, d), vec),
        ],
        out_specs=pl.BlockSpec((ts, d), lambda i, j: (i, 0)),
        out_shape=jax.ShapeDtypeStruct((n, d), _F32),
        scratch_shapes=[
            pltpu.VMEM((ts, d), _BF16),
            pltpu.VMEM((nc, ts, tc), _F32),
            pltpu.VMEM((nc, ts, tc), _F32),
        ],
        compiler_params=_mixer_params(),
        name="sgu_mixer",
    )(x, pre_g, post_g, w_in, w_in, ln_g, ln_b, w_s, b_s_t, w_out)


def _tile_rows(seq_len):
    return min(512, seq_len)


def _run_trunk(x, p):
    b, s, d = x.shape
    ts = _tile_rows(s)
    tc = min(512, d)
    x = x.reshape(b * s, d)
    depth = p["mix_pre_g"].shape[0]
    for i in range(depth):
        kind, j = i % N_MIXERS, i // N_MIXERS
        pre_g, post_g = p["mix_pre_g"][i][None], p["mix_post_g"][i][None]
        if kind == 0:
            x = _pool_mixer(x, pre_g, post_g, p["pool_w_in"][j], p["pool_w_grp"][j],
                            p["pool_scale"][j][None], p["pool_w_out"][j], seq_len=s, ts=ts)
        elif kind == 1:
            x = _sconv_mixer(x, pre_g, post_g, p["sc_w_in"][j], p["sc_conv_w"][j],
                             p["sc_w_out"][j], seq_len=s, ts=ts, tc=tc)
        elif kind == 2:
            x = _conformer_mixer(x, pre_g, post_g, p["cf_w_in"][j], p["cf_dw_w"][j],
                                 p["cf_dw_b"][j][None], p["cf_ln_g"][j][None],
                                 p["cf_ln_b"][j][None], p["cf_w_out"][j], seq_len=s, ts=ts, tc=tc)
        else:
            x = _sgu_mixer(x, pre_g, post_g, p["sg_w_in"][j], p["sg_ln_g"][j][None],
                           p["sg_ln_b"][j][None], p["sg_w_s"][j], p["sg_b_s"][j].T,
                           p["sg_w_out"][j], ts=ts, tc=tc)
        x = _ffn(x, p["ffn_pre_g"][i][None], p["ffn_post_g"][i][None], p["ffn_w_gate"][i],
                 p["ffn_w_up"][i], p["ffn_w_down"][i], tm=ts, tf=min(512, p["ffn_w_gate"].shape[2]))
    return x.reshape(b, s, d)


_MATMUL_WEIGHTS = ("ffn_w_gate", "ffn_w_up", "ffn_w_down", "pool_w_in", "pool_w_grp", "pool_w_out",
                   "sc_w_in", "sc_w_out", "cf_w_in", "cf_w_out", "sg_w_in", "sg_w_s", "sg_w_out")


def kernel(x_prompt, x_sample, mix_pre_g, mix_post_g, ffn_pre_g, ffn_post_g, ffn_w_gate, ffn_w_up,
           ffn_w_down, pool_w_in, pool_w_grp, pool_scale, pool_w_out, sc_w_in, sc_conv_w, sc_w_out,
           cf_w_in, cf_dw_w, cf_dw_b, cf_ln_g, cf_ln_b, cf_w_out, sg_w_in, sg_ln_g, sg_ln_b, sg_w_s,
           sg_b_s, sg_w_out):
    p = dict(mix_pre_g=mix_pre_g, mix_post_g=mix_post_g, ffn_pre_g=ffn_pre_g, ffn_post_g=ffn_post_g,
             ffn_w_gate=ffn_w_gate, ffn_w_up=ffn_w_up, ffn_w_down=ffn_w_down, pool_w_in=pool_w_in,
             pool_w_grp=pool_w_grp, pool_scale=pool_scale, pool_w_out=pool_w_out, sc_w_in=sc_w_in,
             sc_conv_w=sc_conv_w, sc_w_out=sc_w_out, cf_w_in=cf_w_in, cf_dw_w=cf_dw_w,
             cf_dw_b=cf_dw_b, cf_ln_g=cf_ln_g, cf_ln_b=cf_ln_b, cf_w_out=cf_w_out, sg_w_in=sg_w_in,
             sg_ln_g=sg_ln_g, sg_ln_b=sg_ln_b, sg_w_s=sg_w_s, sg_b_s=sg_b_s, sg_w_out=sg_w_out)
    for name in _MATMUL_WEIGHTS:
        p[name] = p[name].astype(_BF16)
    return _run_trunk(x_prompt, p), _run_trunk(x_sample, p)
```

```python
import functools

import jax
import jax.numpy as jnp
from jax import lax
from jax.experimental import pallas as pl
from jax.experimental.pallas import tpu as pltpu

EPS = 1e-6
POOL_WINDOWS = (2, 4, 8, 16)
SC_WIDTH = 3
CF_WIDTH = 31
SG_CHUNK = 128
SG_HEADS = 8
N_MIXERS = 4

HALO = 16
LANES = 128
SUBLANES = 8
VMEM_LIMIT_BYTES = 60 * 1024 * 1024

_BF16 = jnp.bfloat16
_F32 = jnp.float32


def _rms_norm(x, g):
    return x * lax.rsqrt(jnp.mean(x * x, axis=-1, keepdims=True) + EPS) * g


def _dot(a, b):
    return jnp.dot(a, b, preferred_element_type=_F32)


def _residual_out(x_ref, acc, post_g_ref, o_ref):
    o_ref[...] = x_ref[...] + _rms_norm(acc, post_g_ref[...])


NORM_SLICES = 8


def _pre_norm_bf16(x_ref, g_ref):
    rows = x_ref.shape[0] // NORM_SLICES
    g = g_ref[...]
    return jnp.concatenate(
        [_rms_norm(x_ref[pl.ds(k * rows, rows), :], g).astype(_BF16) for k in range(NORM_SLICES)],
        axis=0)


def _fill_xn_ext(xp_ref, x_ref, xn_ref, g_ref, xe_ref, tiles_per_seq):
    t = pl.program_id(0) % tiles_per_seq
    g = g_ref[...]
    prev = jnp.where(t > 0, _rms_norm(xp_ref[...], g), 0.0).astype(_BF16)
    nxt = jnp.where(t < tiles_per_seq - 1, _rms_norm(xn_ref[...], g), 0.0).astype(_BF16)
    xe = jnp.concatenate([prev, _pre_norm_bf16(x_ref, g_ref), nxt], axis=0)
    xe_ref[...] = xe
    return xe


FFN_SUBCHAINS = 2


def _ffn_kernel(x_ref, pre_g_ref, post_g_ref, wg_ref, wu_ref, wd_ref, o_ref, xn_ref, acc_ref):
    j = pl.program_id(1)

    def swiglu_chunk(xn):
        sub = wg_ref.shape[1] // FFN_SUBCHAINS
        down = None
        for s in range(FFN_SUBCHAINS):
            cols = slice(s * sub, (s + 1) * sub)
            gate = _dot(xn, wg_ref[:, cols])
            up = _dot(xn, wu_ref[:, cols])
            a = (gate * jax.nn.sigmoid(gate) * up).astype(_BF16)
            part = _dot(a, wd_ref[cols, :])
            down = part if down is None else down + part
        return down

    @pl.when(j == 0)
    def _():
        xn = _pre_norm_bf16(x_ref, pre_g_ref)
        xn_ref[...] = xn
        acc_ref[...] = swiglu_chunk(xn)

    @pl.when(j > 0)
    def _():
        acc_ref[...] += swiglu_chunk(xn_ref[...])

    @pl.when(j == pl.num_programs(1) - 1)
    def _():
        _residual_out(x_ref, acc_ref[...], post_g_ref, o_ref)


def _ffn(x, pre_g, post_g, wg, wu, wd, layer, *, tm, tf):
    n, d = x.shape
    f = wg.shape[2]
    row = lambda i, j: (i, 0)
    vec = lambda i, j: (0, 0)
    return pl.pallas_call(
        _ffn_kernel,
        grid=(n // tm, f // tf),
        in_specs=[
            pl.BlockSpec((tm, d), row),
            pl.BlockSpec((1, d), vec),
            pl.BlockSpec((1, d), vec),
            pl.BlockSpec((None, d, tf), lambda i, j: (layer, 0, j)),
            pl.BlockSpec((None, d, tf), lambda i, j: (layer, 0, j)),
            pl.BlockSpec((None, tf, d), lambda i, j: (layer, j, 0)),
        ],
        out_specs=pl.BlockSpec((tm, d), row),
        out_shape=jax.ShapeDtypeStruct((n, d), _F32),
        scratch_shapes=[pltpu.VMEM((tm, d), _BF16), pltpu.VMEM((tm, d), _F32)],
        compiler_params=pltpu.CompilerParams(
            dimension_semantics=("arbitrary", "arbitrary"), vmem_limit_bytes=VMEM_LIMIT_BYTES),
        name="ffn",
    )(x, pre_g, post_g, wg, wu, wd)


def _halo_specs(ts, d, n_rows):
    r = ts // HALO
    last = n_rows // HALO - 1
    return [
        pl.BlockSpec((HALO, d), lambda i, j: (jnp.maximum(i * r - 1, 0), 0)),
        pl.BlockSpec((ts, d), lambda i, j: (i, 0)),
        pl.BlockSpec((HALO, d), lambda i, j: (jnp.minimum((i + 1) * r, last), 0)),
    ]


def _mixer_params():
    return pltpu.CompilerParams(
        dimension_semantics=("arbitrary", "arbitrary"), vmem_limit_bytes=VMEM_LIMIT_BYTES)


def _pool_kernel(xp_ref, x_ref, xn_ref, pre_g_ref, post_g_ref, win_ref, wgrp_ref, scale_ref,
                 wout_ref, o_ref, xe_ref, h_ref, ua_ref, ub_ref, acc_ref, *, seq_len):
    ts = x_ref.shape[0]
    ext = ts + 2 * HALO
    tiles_per_seq = seq_len // ts
    j = pl.program_id(1)

    @pl.when((pl.program_id(0) == 0) & (j == 0))
    def _():
        tail = jnp.zeros((SUBLANES, ua_ref.shape[1]), _F32)
        ua_ref[ext - SUBLANES:, :] = tail
        ub_ref[ext - SUBLANES:, :] = tail

    n_lvl = ext - 2 * SUBLANES

    def level(src_ref, dst_ref, n):
        dst_ref[pl.ds(SUBLANES, n_lvl), :] = (src_ref[pl.ds(SUBLANES, n_lvl), :]
                                              + src_ref[pl.ds(SUBLANES + n, n_lvl), :])

    def window_total(w):
        level(h_ref, ua_ref, 1)
        if w == 2:
            return ua_ref[pl.ds(HALO - 1, ts), :]
        level(ua_ref, ub_ref, 2)
        if w == 4:
            return ub_ref[pl.ds(HALO - 2, ts), :]
        level(ub_ref, ua_ref, 4)
        if w == 8:
            return ua_ref[pl.ds(HALO - 4, ts), :]
        return ua_ref[pl.ds(HALO - 8, ts), :] + ua_ref[pl.ds(HALO, ts), :]

    def group(xe, w):
        h_ref[...] = _dot(xe, win_ref[...])
        half = w // 2
        pos = (pl.program_id(0) % tiles_per_seq) * ts + lax.broadcasted_iota(jnp.int32, (ts, 1), 0)
        count = (jnp.minimum(pos + half, seq_len) - jnp.maximum(pos - half, 0)).astype(_F32)
        pooled = window_total(w) / count - h_ref[pl.ds(HALO, ts), :]
        mixed = _dot(pooled.astype(_BF16), wgrp_ref[...]) * scale_ref[...]
        return _dot(mixed.astype(_BF16), wout_ref[...])

    @pl.when(j == 0)
    def _():
        xe = _fill_xn_ext(xp_ref, x_ref, xn_ref, pre_g_ref, xe_ref, tiles_per_seq)
        acc_ref[...] = group(xe, POOL_WINDOWS[0])

    for g in range(1, len(POOL_WINDOWS)):
        @pl.when(j == g)
        def _(g=g):
            acc_ref[...] += group(xe_ref[...], POOL_WINDOWS[g])

    @pl.when(j == pl.num_programs(1) - 1)
    def _():
        _residual_out(x_ref, acc_ref[...], post_g_ref, o_ref)


def _pool_mixer(x, pre_g, post_g, w_in, w_grp, scale, w_out, *, seq_len, ts):
    n, d = x.shape
    ng, gd = w_grp.shape[0], w_grp.shape[1]
    vec = lambda i, j: (0, 0)
    assert POOL_WINDOWS == (2, 4, 8, 16) and ng == len(POOL_WINDOWS)
    return pl.pallas_call(
        functools.partial(_pool_kernel, seq_len=seq_len),
        grid=(n // ts, ng),
        in_specs=_halo_specs(ts, d, n) + [
            pl.BlockSpec((1, d), vec),
            pl.BlockSpec((1, d), vec),
            pl.BlockSpec((d, gd), lambda i, j: (0, j)),
            pl.BlockSpec((None, gd, gd), lambda i, j: (j, 0, 0)),
            pl.BlockSpec((1, gd), lambda i, j: (0, j)),
            pl.BlockSpec((gd, d), lambda i, j: (j, 0)),
        ],
        out_specs=pl.BlockSpec((ts, d), lambda i, j: (i, 0)),
        out_shape=jax.ShapeDtypeStruct((n, d), _F32),
        scratch_shapes=[
            pltpu.VMEM((ts + 2 * HALO, d), _BF16),
            pltpu.VMEM((ts + 2 * HALO, gd), _F32),
            pltpu.VMEM((ts + 2 * HALO, gd), _F32),
            pltpu.VMEM((ts + 2 * HALO, gd), _F32),
            pltpu.VMEM((ts, d), _F32),
        ],
        compiler_params=_mixer_params(),
        name="pool_mixer",
    )(x, x, x, pre_g, post_g, w_in, w_grp, scale, w_out)


def _sconv_kernel(xp_ref, x_ref, xn_ref, pre_g_ref, post_g_ref, wb_ref, wc_ref, wh_ref, cw_ref,
                  wout_ref, o_ref, xe_ref, p_ref, acc_ref, *, seq_len):
    ts = x_ref.shape[0]
    j = pl.program_id(1)

    def chunk(xe):
        p_ref[...] = _dot(xe, wc_ref[...]) * _dot(xe, wh_ref[...])
        half = SC_WIDTH // 2
        y = cw_ref[0:1, :] * p_ref[pl.ds(HALO - half, ts), :]
        for k in range(1, SC_WIDTH):
            y = y + cw_ref[k:k + 1, :] * p_ref[pl.ds(HALO - half + k, ts), :]
        gb = _dot(xe[HALO:HALO + ts, :], wb_ref[...])
        return _dot((gb * y).astype(_BF16), wout_ref[...])

    @pl.when(j == 0)
    def _():
        xe = _fill_xn_ext(xp_ref, x_ref, xn_ref, pre_g_ref, xe_ref, seq_len // ts)
        acc_ref[...] = chunk(xe)

    @pl.when(j > 0)
    def _():
        acc_ref[...] += chunk(xe_ref[...])

    @pl.when(j == pl.num_programs(1) - 1)
    def _():
        _residual_out(x_ref, acc_ref[...], post_g_ref, o_ref)


def _sconv_mixer(x, pre_g, post_g, w_in, conv_w, w_out, *, seq_len, ts, tc):
    n, d = x.shape
    nc = d // tc
    vec = lambda i, j: (0, 0)
    return pl.pallas_call(
        functools.partial(_sconv_kernel, seq_len=seq_len),
        grid=(n // ts, nc),
        in_specs=_halo_specs(ts, d, n) + [
            pl.BlockSpec((1, d), vec),
            pl.BlockSpec((1, d), vec),
            pl.BlockSpec((d, tc), lambda i, j: (0, j)),
            pl.BlockSpec((d, tc), lambda i, j: (0, nc + j)),
            pl.BlockSpec((d, tc), lambda i, j: (0, 2 * nc + j)),
            pl.BlockSpec((SC_WIDTH, tc), lambda i, j: (0, j)),
            pl.BlockSpec((tc, d), lambda i, j: (j, 0)),
        ],
        out_specs=pl.BlockSpec((ts, d), lambda i, j: (i, 0)),
        out_shape=jax.ShapeDtypeStruct((n, d), _F32),
        scratch_shapes=[
            pltpu.VMEM((ts + 2 * HALO, d), _BF16),
            pltpu.VMEM((ts + 2 * HALO, tc), _F32),
            pltpu.VMEM((ts, d), _F32),
        ],
        compiler_params=_mixer_params(),
        name="sconv_mixer",
    )(x, x, x, pre_g, post_g, w_in, w_in, w_in, conv_w, w_out)


def _layer_norm_stats(y_ref):
    nc, _, tc = y_ref.shape
    inv_d = 1.0 / (nc * tc)
    mu = sum(jnp.sum(y_ref[c], axis=-1, keepdims=True) for c in range(nc)) * inv_d
    var = sum(jnp.sum(jnp.square(y_ref[c] - mu), axis=-1, keepdims=True) for c in range(nc)) * inv_d
    return mu, lax.rsqrt(var + EPS)


CONV_ROWS = 64


def _glu_proj(xe, wa_ref, wgate_ref, p_ref):
    p_ref[...] = _dot(xe, wa_ref[...]) * jax.nn.sigmoid(_dot(xe, wgate_ref[...]))


def _depthwise_conv(p_ref, w_ref, b_ref, c, sh_ref, y_ref, ts):
    tc = p_ref.shape[1]
    n_sh = sh_ref.shape[2]
    for l in range(tc // LANES):
        lanes = pl.ds(l * LANES, LANES)
        sh = sh_ref.at[l % 2]
        for r in range(1, SUBLANES):
            sh[r - 1] = p_ref[pl.ds(r, n_sh), lanes]
        for rb in range(ts // CONV_ROWS):
            acc = None
            for r in range(SUBLANES):
                taps = [(k, (HALO - CF_WIDTH // 2 + k) // SUBLANES) for k in range(CF_WIDTH)
                        if (HALO - CF_WIDTH // 2 + k) % SUBLANES == r]
                a_lo, a_hi = taps[0][1], taps[-1][1]
                rows = pl.ds(a_lo * SUBLANES + rb * CONV_ROWS, CONV_ROWS + (a_hi - a_lo) * SUBLANES)
                span = p_ref[rows, lanes] if r == 0 else sh[r - 1, rows, :]
                for k, a in taps:
                    off = (a - a_lo) * SUBLANES
                    term = w_ref[c, k:k + 1, lanes] * span[off:off + CONV_ROWS, :]
                    acc = term if acc is None else acc + term
            y_ref[c, pl.ds(rb * CONV_ROWS, CONV_ROWS), lanes] = acc + b_ref[c, :, lanes]


def _conformer_kernel(xp_ref, x_ref, xn_ref, pre_g_ref, post_g_ref, wa_ref, wgate_ref, dww_ref,
                      dwb_ref, lng_ref, lnb_ref, wout_ref, o_ref, xe_ref, p0_ref, p1_ref, sh_ref,
                      y_ref, *, seq_len):
    ts = x_ref.shape[0]
    nc, _, tc = y_ref.shape
    j = pl.program_id(1)
    p_refs = (p0_ref, p1_ref)

    @pl.when(j == 0)
    def _():
        xe = _fill_xn_ext(xp_ref, x_ref, xn_ref, pre_g_ref, xe_ref, seq_len // ts)
        _glu_proj(xe, wa_ref, wgate_ref, p0_ref)

    for parity in (0, 1):
        @pl.when((j > 0) & (j % 2 == parity))
        def _(parity=parity):
            _depthwise_conv(p_refs[1 - parity], dww_ref, dwb_ref, j - 1, sh_ref, y_ref, ts)
            _glu_proj(xe_ref[...], wa_ref, wgate_ref, p_refs[parity])

    @pl.when(j == nc - 1)
    def _():
        _depthwise_conv(p_refs[(nc - 1) % 2], dww_ref, dwb_ref, nc - 1, sh_ref, y_ref, ts)

    @pl.when(j >= nc - 1)
    def _():
        mu, rstd = _layer_norm_stats(y_ref)
        acc = jnp.zeros((ts, wout_ref.shape[1]), _F32)
        for c in range(nc):
            cols = slice(c * tc, (c + 1) * tc)
            z = (y_ref[c] - mu) * rstd * lng_ref[:, cols] + lnb_ref[:, cols]
            z = z * jax.nn.sigmoid(z)
            acc = acc + _dot(z.astype(_BF16), wout_ref[cols, :])
        _residual_out(x_ref, acc, post_g_ref, o_ref)


def _conformer_mixer(x, pre_g, post_g, w_in, dw_w, dw_b, ln_g, ln_b, w_out, *, seq_len, ts, tc):
    n, d = x.shape
    nc = d // tc
    ext = ts + 2 * HALO
    vec = lambda i, j: (0, 0)
    vec3 = lambda i, j: (0, 0, 0)
    dw_w = dw_w.reshape(CF_WIDTH, nc, tc).transpose(1, 0, 2)
    dw_b = dw_b.reshape(nc, 1, tc)
    return pl.pallas_call(
        functools.partial(_conformer_kernel, seq_len=seq_len),
        grid=(n // ts, nc),
        in_specs=_halo_specs(ts, d, n) + [
            pl.BlockSpec((1, d), vec),
            pl.BlockSpec((1, d), vec),
            pl.BlockSpec((d, tc), lambda i, j: (0, j)),
            pl.BlockSpec((d, tc), lambda i, j: (0, nc + j)),
            pl.BlockSpec((nc, CF_WIDTH, tc), vec3),
            pl.BlockSpec((nc, 1, tc), vec3),
            pl.BlockSpec((1, d), vec),
            pl.BlockSpec((1, d), vec),
            pl.BlockSpec((d, d), vec, pipeline_mode=pl.Buffered(1)),
        ],
        out_specs=pl.BlockSpec((ts, d), lambda i, j: (i, 0)),
        out_shape=jax.ShapeDtypeStruct((n, d), _F32),
        scratch_shapes=[
            pltpu.VMEM((ext, d), _BF16),
            pltpu.VMEM((ext, tc), _F32),
            pltpu.VMEM((ext, tc), _F32),
            pltpu.VMEM((2, SUBLANES - 1, ext - SUBLANES, LANES), _F32),
            pltpu.VMEM((nc, ts, tc), _F32),
        ],
        compiler_params=_mixer_params(),
        name="conformer_mixer",
    )(x, x, x, pre_g, post_g, w_in, w_in, dw_w, dw_b, ln_g, ln_b, w_out)


def _gelu(x):
    return 0.5 * x * (1.0 + lax.erf(x * (2.0 ** -0.5)))


def _sgu_kernel(x_ref, pre_g_ref, post_g_ref, wu_ref, wv_ref, lng_ref, lnb_ref, ws_ref, bs_ref,
                wout_ref, o_ref, xn_ref, u_ref, v_ref):
    ts = x_ref.shape[0]
    nc, _, tc = v_ref.shape
    n_heads = ws_ref.shape[0]
    hd = (nc * tc) // n_heads
    j = pl.program_id(1)

    def project(xn):
        u_ref[j] = _gelu(_dot(xn, wu_ref[...]))
        v_ref[j] = _gelu(_dot(xn, wv_ref[...]))

    @pl.when(j == 0)
    def _():
        xn = _pre_norm_bf16(x_ref, pre_g_ref)
        xn_ref[...] = xn
        project(xn)

    @pl.when(j > 0)
    def _():
        project(xn_ref[...])

    @pl.when(j == nc - 1)
    def _():
        mu, rstd = _layer_norm_stats(v_ref)
        acc = jnp.zeros((ts, wout_ref.shape[1]), _F32)
        for c in range(nc):
            cols = slice(c * tc, (c + 1) * tc)
            vn = ((v_ref[c] - mu) * rstd * lng_ref[:, cols] + lnb_ref[:, cols]).astype(_BF16)
            gated = []
            for r in range(ts // SG_CHUNK):
                rows = slice(r * SG_CHUNK, (r + 1) * SG_CHUNK)
                parts = []
                for hh in range(tc // hd):
                    head = c * (tc // hd) + hh
                    s = _dot(ws_ref[head], vn[rows, hh * hd:(hh + 1) * hd])
                    parts.append(s + bs_ref[:, head:head + 1])
                gated.append(jnp.concatenate(parts, axis=1) if len(parts) > 1 else parts[0])
            sv = jnp.concatenate(gated, axis=0) if len(gated) > 1 else gated[0]
            acc = acc + _dot((u_ref[c] * sv).astype(_BF16), wout_ref[cols, :])
        _residual_out(x_ref, acc, post_g_ref, o_ref)


def _sgu_mixer(x, pre_g, post_g, w_in, ln_g, ln_b, w_s, b_s_t, w_out, *, ts, tc):
    n, d = x.shape
    nc = d // tc
    vec = lambda i, j: (0, 0)
    return pl.pallas_call(
        _sgu_kernel,
        grid=(n // ts, nc),
        in_specs=[
            pl.BlockSpec((ts, d), lambda i, j: (i, 0)),
            pl.BlockSpec((1, d), vec),
            pl.BlockSpec((1, d), vec),
            pl.BlockSpec((d, tc), lambda i, j: (0, j)),
            pl.BlockSpec((d, tc), lambda i, j: (0, nc + j)),
            pl.BlockSpec((1, d), vec),
            pl.BlockSpec((1, d), vec),
            pl.BlockSpec(w_s.shape, lambda i, j: (0, 0, 0)),
            pl.BlockSpec(b_s_t.shape, vec),
            pl.BlockSpec((d, d), vec),
        ],
        out_specs=pl.BlockSpec((ts, d), lambda i, j: (i, 0)),
        out_shape=jax.ShapeDtypeStruct((n, d), _F32),
        scratch_shapes=[
            pltpu.VMEM((ts, d), _BF16),
            pltpu.VMEM((nc, ts, tc), _F32),
            pltpu.VMEM((nc, ts, tc), _F32),
        ],
        compiler_params=_mixer_params(),
        name="sgu_mixer",
    )(x, pre_g, post_g, w_in, w_in, ln_g, ln_b, w_s, b_s_t, w_out)


def _tile_rows(seq_len):
    return min(512, seq_len)


def _run_trunk(x, p):
    b, s, d = x.shape
    ts = _tile_rows(s)
    tc = min(512, d)
    x = x.reshape(b * s, d)
    depth = p["mix_pre_g"].shape[0]
    for i in range(depth):
        kind, j = i % N_MIXERS, i // N_MIXERS
        pre_g, post_g = p["mix_pre_g"][i][None], p["mix_post_g"][i][None]
        if kind == 0:
            x = _pool_mixer(x, pre_g, post_g, p["pool_w_in"][j], p["pool_w_grp"][j],
                            p["pool_scale"][j][None], p["pool_w_out"][j], seq_len=s, ts=ts)
        elif kind == 1:
            x = _sconv_mixer(x, pre_g, post_g, p["sc_w_in"][j], p["sc_conv_w"][j],
                             p["sc_w_out"][j], seq_len=s, ts=ts, tc=tc)
        elif kind == 2:
            x = _conformer_mixer(x, pre_g, post_g, p["cf_w_in"][j], p["cf_dw_w"][j],
                                 p["cf_dw_b"][j][None], p["cf_ln_g"][j][None],
                                 p["cf_ln_b"][j][None], p["cf_w_out"][j], seq_len=s, ts=ts, tc=tc)
        else:
            x = _sgu_mixer(x, pre_g, post_g, p["sg_w_in"][j], p["sg_ln_g"][j][None],
                           p["sg_ln_b"][j][None], p["sg_w_s"][j], p["sg_b_s"][j].T,
                           p["sg_w_out"][j], ts=ts, tc=tc)
        x = _ffn(x, p["ffn_pre_g"][i][None], p["ffn_post_g"][i][None], p["ffn_w_gate"],
                 p["ffn_w_up"], p["ffn_w_down"], i, tm=ts, tf=min(512, p["ffn_w_gate"].shape[2]))
    return x.reshape(b, s, d)


_MATMUL_WEIGHTS = ("ffn_w_gate", "ffn_w_up", "ffn_w_down", "pool_w_in", "pool_w_grp", "pool_w_out",
                   "sc_w_in", "sc_w_out", "cf_w_in", "cf_w_out", "sg_w_in", "sg_w_s", "sg_w_out")


def kernel(x_prompt, x_sample, mix_pre_g, mix_post_g, ffn_pre_g, ffn_post_g, ffn_w_gate, ffn_w_up,
           ffn_w_down, pool_w_in, pool_w_grp, pool_scale, pool_w_out, sc_w_in, sc_conv_w, sc_w_out,
           cf_w_in, cf_dw_w, cf_dw_b, cf_ln_g, cf_ln_b, cf_w_out, sg_w_in, sg_ln_g, sg_ln_b, sg_w_s,
           sg_b_s, sg_w_out):
    p = dict(mix_pre_g=mix_pre_g, mix_post_g=mix_post_g, ffn_pre_g=ffn_pre_g, ffn_post_g=ffn_post_g,
             ffn_w_gate=ffn_w_gate, ffn_w_up=ffn_w_up, ffn_w_down=ffn_w_down, pool_w_in=pool_w_in,
             pool_w_grp=pool_w_grp, pool_scale=pool_scale, pool_w_out=pool_w_out, sc_w_in=sc_w_in,
             sc_conv_w=sc_conv_w, sc_w_out=sc_w_out, cf_w_in=cf_w_in, cf_dw_w=cf_dw_w,
             cf_dw_b=cf_dw_b, cf_ln_g=cf_ln_g, cf_ln_b=cf_ln_b, cf_w_out=cf_w_out, sg_w_in=sg_w_in,
             sg_ln_g=sg_ln_g, sg_ln_b=sg_ln_b, sg_w_s=sg_w_s, sg_b_s=sg_b_s, sg_w_out=sg_w_out)
    for name in _MATMUL_WEIGHTS:
        p[name] = p[name].astype(_BF16)
    return _run_trunk(x_prompt, p), _run_trunk(x_sample, p)
```

```python
import functools

import jax
import jax.numpy as jnp
from jax import lax
from jax.experimental import pallas as pl
from jax.experimental.pallas import tpu as pltpu

EPS = 1e-6
POOL_WINDOWS = (2, 4, 8, 16)
SC_WIDTH = 3
CF_WIDTH = 31
SG_CHUNK = 128
SG_HEADS = 8
N_MIXERS = 4

HALO = 16
LANES = 128
SUBLANES = 8
VMEM_LIMIT_BYTES = 60 * 1024 * 1024

_BF16 = jnp.bfloat16
_F32 = jnp.float32


def _rms_norm(x, g):
    return x * lax.rsqrt(jnp.mean(x * x, axis=-1, keepdims=True) + EPS) * g


def _dot(a, b):
    return jnp.dot(a, b, preferred_element_type=_F32)


def _residual_out(x_ref, acc, post_g_ref, o_ref):
    o_ref[...] = x_ref[...] + _rms_norm(acc, post_g_ref[...])


NORM_SLICES = 8


def _pre_norm_bf16(x_ref, g_ref):
    rows = x_ref.shape[0] // NORM_SLICES
    g = g_ref[...]
    return jnp.concatenate(
        [_rms_norm(x_ref[pl.ds(k * rows, rows), :], g).astype(_BF16) for k in range(NORM_SLICES)],
        axis=0)


def _fill_xn_ext(xp_ref, x_ref, xn_ref, g_ref, xe_ref, tiles_per_seq):
    t = pl.program_id(0) % tiles_per_seq
    g = g_ref[...]
    prev = jnp.where(t > 0, _rms_norm(xp_ref[...], g), 0.0).astype(_BF16)
    nxt = jnp.where(t < tiles_per_seq - 1, _rms_norm(xn_ref[...], g), 0.0).astype(_BF16)
    xe = jnp.concatenate([prev, _pre_norm_bf16(x_ref, g_ref), nxt], axis=0)
    xe_ref[...] = xe
    return xe


FFN_SUBCHAIN_COLS = 512
FFN_WIDE_COLS = 1024


def _ffn_kernel(x_ref, pre_g_ref, post_g_ref, wg_ref, wu_ref, wd_ref, wg_tail_ref, wu_tail_ref,
                wd_tail_ref, o_ref, xn_ref, acc_ref):
    j = pl.program_id(1)
    last = pl.num_programs(1) - 1

    def swiglu_chunk(xn, wg, wu, wd):
        down = None
        for c in range(0, wg.shape[1], FFN_SUBCHAIN_COLS):
            cols = slice(c, c + FFN_SUBCHAIN_COLS)
            gate = _dot(xn, wg[:, cols])
            up = _dot(xn, wu[:, cols])
            a = (gate * jax.nn.sigmoid(gate) * up).astype(_BF16)
            part = _dot(a, wd[cols, :])
            down = part if down is None else down + part
        return down

    @pl.when(j == 0)
    def _():
        xn = _pre_norm_bf16(x_ref, pre_g_ref)
        xn_ref[...] = xn
        acc_ref[...] = swiglu_chunk(xn, wg_ref, wu_ref, wd_ref)

    @pl.when((j > 0) & (j < last))
    def _():
        acc_ref[...] += swiglu_chunk(xn_ref[...], wg_ref, wu_ref, wd_ref)

    @pl.when(j == last)
    def _():
        acc = acc_ref[...] + swiglu_chunk(xn_ref[...], wg_tail_ref, wu_tail_ref, wd_tail_ref)
        _residual_out(x_ref, acc, post_g_ref, o_ref)


def _ffn(x, pre_g, post_g, wg, wu, wd, layer, *, tm):
    n, d = x.shape
    f = wg.shape[2]
    n_wide = (f - 1) // FFN_WIDE_COLS
    tail = f - n_wide * FFN_WIDE_COLS
    assert n_wide >= 1 and tail % FFN_SUBCHAIN_COLS == 0 and (n_wide * FFN_WIDE_COLS) % tail == 0
    tail_blk = n_wide * FFN_WIDE_COLS // tail
    row = lambda i, j: (i, 0)
    vec = lambda i, j: (0, 0)
    wide = lambda i, j: jnp.minimum(j, n_wide - 1)
    resident = pl.Buffered(1)
    return pl.pallas_call(
        _ffn_kernel,
        grid=(n // tm, n_wide + 1),
        in_specs=[
            pl.BlockSpec((tm, d), row),
            pl.BlockSpec((1, d), vec),
            pl.BlockSpec((1, d), vec),
            pl.BlockSpec((None, d, FFN_WIDE_COLS), lambda i, j: (layer, 0, wide(i, j))),
            pl.BlockSpec((None, d, FFN_WIDE_COLS), lambda i, j: (layer, 0, wide(i, j))),
            pl.BlockSpec((None, FFN_WIDE_COLS, d), lambda i, j: (layer, wide(i, j), 0)),
            pl.BlockSpec((None, d, tail), lambda i, j: (layer, 0, tail_blk), pipeline_mode=resident),
            pl.BlockSpec((None, d, tail), lambda i, j: (layer, 0, tail_blk), pipeline_mode=resident),
            pl.BlockSpec((None, tail, d), lambda i, j: (layer, tail_blk, 0), pipeline_mode=resident),
        ],
        out_specs=pl.BlockSpec((tm, d), row),
        out_shape=jax.ShapeDtypeStruct((n, d), _F32),
        scratch_shapes=[pltpu.VMEM((tm, d), _BF16), pltpu.VMEM((tm, d), _F32)],
        compiler_params=pltpu.CompilerParams(
            dimension_semantics=("arbitrary", "arbitrary"), vmem_limit_bytes=VMEM_LIMIT_BYTES),
        name="ffn",
    )(x, pre_g, post_g, wg, wu, wd, wg, wu, wd)


def _halo_specs(ts, d, n_rows):
    r = ts // HALO
    last = n_rows // HALO - 1
    return [
        pl.BlockSpec((HALO, d), lambda i, j: (jnp.maximum(i * r - 1, 0), 0)),
        pl.BlockSpec((ts, d), lambda i, j: (i, 0)),
        pl.BlockSpec((HALO, d), lambda i, j: (jnp.minimum((i + 1) * r, last), 0)),
    ]


def _mixer_params():
    return pltpu.CompilerParams(
        dimension_semantics=("arbitrary", "arbitrary"), vmem_limit_bytes=VMEM_LIMIT_BYTES)


def _pool_kernel(xp_ref, x_ref, xn_ref, pre_g_ref, post_g_ref, win_ref, wgrp_ref, scale_ref,
                 wout_ref, o_ref, xe_ref, h_ref, ua_ref, ub_ref, acc_ref, *, seq_len):
    ts = x_ref.shape[0]
    ext = ts + 2 * HALO
    tiles_per_seq = seq_len // ts
    j = pl.program_id(1)

    @pl.when((pl.program_id(0) == 0) & (j == 0))
    def _():
        tail = jnp.zeros((SUBLANES, ua_ref.shape[1]), _F32)
        ua_ref[ext - SUBLANES:, :] = tail
        ub_ref[ext - SUBLANES:, :] = tail

    n_lvl = ext - 2 * SUBLANES

    def level(src_ref, dst_ref, n):
        dst_ref[pl.ds(SUBLANES, n_lvl), :] = (src_ref[pl.ds(SUBLANES, n_lvl), :]
                                              + src_ref[pl.ds(SUBLANES + n, n_lvl), :])

    def window_total(w):
        level(h_ref, ua_ref, 1)
        if w == 2:
            return ua_ref[pl.ds(HALO - 1, ts), :]
        level(ua_ref, ub_ref, 2)
        if w == 4:
            return ub_ref[pl.ds(HALO - 2, ts), :]
        level(ub_ref, ua_ref, 4)
        if w == 8:
            return ua_ref[pl.ds(HALO - 4, ts), :]
        return ua_ref[pl.ds(HALO - 8, ts), :] + ua_ref[pl.ds(HALO, ts), :]

    def group(xe, w):
        h_ref[...] = _dot(xe, win_ref[...])
        half = w // 2
        pos = (pl.program_id(0) % tiles_per_seq) * ts + lax.broadcasted_iota(jnp.int32, (ts, 1), 0)
        count = (jnp.minimum(pos + half, seq_len) - jnp.maximum(pos - half, 0)).astype(_F32)
        pooled = window_total(w) / count - h_ref[pl.ds(HALO, ts), :]
        mixed = _dot(pooled.astype(_BF16), wgrp_ref[...]) * scale_ref[...]
        return _dot(mixed.astype(_BF16), wout_ref[...])

    @pl.when(j == 0)
    def _():
        xe = _fill_xn_ext(xp_ref, x_ref, xn_ref, pre_g_ref, xe_ref, tiles_per_seq)
        acc_ref[...] = group(xe, POOL_WINDOWS[0])

    for g in range(1, len(POOL_WINDOWS)):
        @pl.when(j == g)
        def _(g=g):
            acc_ref[...] += group(xe_ref[...], POOL_WINDOWS[g])

    @pl.when(j == pl.num_programs(1) - 1)
    def _():
        _residual_out(x_ref, acc_ref[...], post_g_ref, o_ref)


def _pool_mixer(x, pre_g, post_g, w_in, w_grp, scale, w_out, *, seq_len, ts):
    n, d = x.shape
    ng, gd = w_grp.shape[0], w_grp.shape[1]
    vec = lambda i, j: (0, 0)
    assert POOL_WINDOWS == (2, 4, 8, 16) and ng == len(POOL_WINDOWS)
    return pl.pallas_call(
        functools.partial(_pool_kernel, seq_len=seq_len),
        grid=(n // ts, ng),
        in_specs=_halo_specs(ts, d, n) + [
            pl.BlockSpec((1, d), vec),
            pl.BlockSpec((1, d), vec),
            pl.BlockSpec((d, gd), lambda i, j: (0, j)),
            pl.BlockSpec((None, gd, gd), lambda i, j: (j, 0, 0)),
            pl.BlockSpec((1, gd), lambda i, j: (0, j)),
            pl.BlockSpec((gd, d), lambda i, j: (j, 0)),
        ],
        out_specs=pl.BlockSpec((ts, d), lambda i, j: (i, 0)),
        out_shape=jax.ShapeDtypeStruct((n, d), _F32),
        scratch_shapes=[
            pltpu.VMEM((ts + 2 * HALO, d), _BF16),
            pltpu.VMEM((ts + 2 * HALO, gd), _F32),
            pltpu.VMEM((ts + 2 * HALO, gd), _F32),
            pltpu.VMEM((ts + 2 * HALO, gd), _F32),
            pltpu.VMEM((ts, d), _F32),
        ],
        compiler_params=_mixer_params(),
        name="pool_mixer",
    )(x, x, x, pre_g, post_g, w_in, w_grp, scale, w_out)


def _sconv_kernel(xp_ref, x_ref, xn_ref, pre_g_ref, post_g_ref, wb_ref, wc_ref, wh_ref, cw_ref,
                  wout_ref, o_ref, xe_ref, p_ref, acc_ref, *, seq_len):
    ts = x_ref.shape[0]
    j = pl.program_id(1)

    def chunk(xe):
        p_ref[...] = _dot(xe, wc_ref[...]) * _dot(xe, wh_ref[...])
        half = SC_WIDTH // 2
        y = cw_ref[0:1, :] * p_ref[pl.ds(HALO - half, ts), :]
        for k in range(1, SC_WIDTH):
            y = y + cw_ref[k:k + 1, :] * p_ref[pl.ds(HALO - half + k, ts), :]
        gb = _dot(xe[HALO:HALO + ts, :], wb_ref[...])
        return _dot((gb * y).astype(_BF16), wout_ref[...])

    @pl.when(j == 0)
    def _():
        xe = _fill_xn_ext(xp_ref, x_ref, xn_ref, pre_g_ref, xe_ref, seq_len // ts)
        acc_ref[...] = chunk(xe)

    @pl.when(j > 0)
    def _():
        acc_ref[...] += chunk(xe_ref[...])

    @pl.when(j == pl.num_programs(1) - 1)
    def _():
        _residual_out(x_ref, acc_ref[...], post_g_ref, o_ref)


def _sconv_mixer(x, pre_g, post_g, w_in, conv_w, w_out, *, seq_len, ts, tc):
    n, d = x.shape
    nc = d // tc
    vec = lambda i, j: (0, 0)
    return pl.pallas_call(
        functools.partial(_sconv_kernel, seq_len=seq_len),
        grid=(n // ts, nc),
        in_specs=_halo_specs(ts, d, n) + [
            pl.BlockSpec((1, d), vec),
            pl.BlockSpec((1, d), vec),
            pl.BlockSpec((d, tc), lambda i, j: (0, j)),
            pl.BlockSpec((d, tc), lambda i, j: (0, nc + j)),
            pl.BlockSpec((d, tc), lambda i, j: (0, 2 * nc + j)),
            pl.BlockSpec((SC_WIDTH, tc), lambda i, j: (0, j)),
            pl.BlockSpec((tc, d), lambda i, j: (j, 0)),
        ],
        out_specs=pl.BlockSpec((ts, d), lambda i, j: (i, 0)),
        out_shape=jax.ShapeDtypeStruct((n, d), _F32),
        scratch_shapes=[
            pltpu.VMEM((ts + 2 * HALO, d), _BF16),
            pltpu.VMEM((ts + 2 * HALO, tc), _F32),
            pltpu.VMEM((ts, d), _F32),
        ],
        compiler_params=_mixer_params(),
        name="sconv_mixer",
    )(x, x, x, pre_g, post_g, w_in, w_in, w_in, conv_w, w_out)


def _layer_norm_stats(y_ref):
    nc, _, tc = y_ref.shape
    inv_d = 1.0 / (nc * tc)
    mu = sum(jnp.sum(y_ref[c], axis=-1, keepdims=True) for c in range(nc)) * inv_d
    var = sum(jnp.sum(jnp.square(y_ref[c] - mu), axis=-1, keepdims=True) for c in range(nc)) * inv_d
    return mu, lax.rsqrt(var + EPS)


CONV_ROWS = 64


def _glu_proj(xe, wa_ref, wgate_ref, p_ref):
    p_ref[...] = _dot(xe, wa_ref[...]) * jax.nn.sigmoid(_dot(xe, wgate_ref[...]))


def _depthwise_conv(p_ref, w_ref, b_ref, c, sh_ref, y_ref, ts):
    tc = p_ref.shape[1]
    n_sh = sh_ref.shape[2]
    for l in range(tc // LANES):
        lanes = pl.ds(l * LANES, LANES)
        sh = sh_ref.at[l % 2]
        for r in range(1, SUBLANES):
            sh[r - 1] = p_ref[pl.ds(r, n_sh), lanes]
        for rb in range(ts // CONV_ROWS):
            acc = None
            for r in range(SUBLANES):
                taps = [(k, (HALO - CF_WIDTH // 2 + k) // SUBLANES) for k in range(CF_WIDTH)
                        if (HALO - CF_WIDTH // 2 + k) % SUBLANES == r]
                a_lo, a_hi = taps[0][1], taps[-1][1]
                rows = pl.ds(a_lo * SUBLANES + rb * CONV_ROWS, CONV_ROWS + (a_hi - a_lo) * SUBLANES)
                span = p_ref[rows, lanes] if r == 0 else sh[r - 1, rows, :]
                for k, a in taps:
                    off = (a - a_lo) * SUBLANES
                    term = w_ref[c, k:k + 1, lanes] * span[off:off + CONV_ROWS, :]
                    acc = term if acc is None else acc + term
            y_ref[c, pl.ds(rb * CONV_ROWS, CONV_ROWS), lanes] = acc + b_ref[c, :, lanes]


def _conformer_kernel(xp_ref, x_ref, xn_ref, pre_g_ref, post_g_ref, wa_ref, wgate_ref, dww_ref,
                      dwb_ref, lng_ref, lnb_ref, wout_ref, o_ref, xe_ref, p0_ref, p1_ref, sh_ref,
                      y_ref, *, seq_len):
    ts = x_ref.shape[0]
    nc, _, tc = y_ref.shape
    j = pl.program_id(1)
    p_refs = (p0_ref, p1_ref)

    @pl.when(j == 0)
    def _():
        xe = _fill_xn_ext(xp_ref, x_ref, xn_ref, pre_g_ref, xe_ref, seq_len // ts)
        _glu_proj(xe, wa_ref, wgate_ref, p0_ref)

    for parity in (0, 1):
        @pl.when((j > 0) & (j % 2 == parity))
        def _(parity=parity):
            _depthwise_conv(p_refs[1 - parity], dww_ref, dwb_ref, j - 1, sh_ref, y_ref, ts)
            _glu_proj(xe_ref[...], wa_ref, wgate_ref, p_refs[parity])

    @pl.when(j == nc - 1)
    def _():
        _depthwise_conv(p_refs[(nc - 1) % 2], dww_ref, dwb_ref, nc - 1, sh_ref, y_ref, ts)

    @pl.when(j >= nc - 1)
    def _():
        mu, rstd = _layer_norm_stats(y_ref)
        acc = jnp.zeros((ts, wout_ref.shape[1]), _F32)
        for c in range(nc):
            cols = slice(c * tc, (c + 1) * tc)
            z = (y_ref[c] - mu) * rstd * lng_ref[:, cols] + lnb_ref[:, cols]
            z = z * jax.nn.sigmoid(z)
            acc = acc + _dot(z.astype(_BF16), wout_ref[cols, :])
        _residual_out(x_ref, acc, post_g_ref, o_ref)


def _conformer_mixer(x, pre_g, post_g, w_in, dw_w, dw_b, ln_g, ln_b, w_out, *, seq_len, ts, tc):
    n, d = x.shape
    nc = d // tc
    ext = ts + 2 * HALO
    vec = lambda i, j: (0, 0)
    vec3 = lambda i, j: (0, 0, 0)
    dw_w = dw_w.reshape(CF_WIDTH, nc, tc).transpose(1, 0, 2)
    dw_b = dw_b.reshape(nc, 1, tc)
    return pl.pallas_call(
        functools.partial(_conformer_kernel, seq_len=seq_len),
        grid=(n // ts, nc),
        in_specs=_halo_specs(ts, d, n) + [
            pl.BlockSpec((1, d), vec),
            pl.BlockSpec((1, d), vec),
            pl.BlockSpec((d, tc), lambda i, j: (0, j)),
            pl.BlockSpec((d, tc), lambda i, j: (0, nc + j)),
            pl.BlockSpec((nc, CF_WIDTH, tc), vec3),
            pl.BlockSpec((nc, 1, tc), vec3),
            pl.BlockSpec((1, d), vec),
            pl.BlockSpec((1, d), vec),
            pl.BlockSpec((d, d), vec, pipeline_mode=pl.Buffered(1)),
        ],
        out_specs=pl.BlockSpec((ts, d), lambda i, j: (i, 0)),
        out_shape=jax.ShapeDtypeStruct((n, d), _F32),
        scratch_shapes=[
            pltpu.VMEM((ext, d), _BF16),
            pltpu.VMEM((ext, tc), _F32),
            pltpu.VMEM((ext, tc), _F32),
            pltpu.VMEM((2, SUBLANES - 1, ext - SUBLANES, LANES), _F32),
            pltpu.VMEM((nc, ts, tc), _F32),
        ],
        compiler_params=_mixer_params(),
        name="conformer_mixer",
    )(x, x, x, pre_g, post_g, w_in, w_in, dw_w, dw_b, ln_g, ln_b, w_out)


def _gelu(x):
    return 0.5 * x * (1.0 + lax.erf(x * (2.0 ** -0.5)))


def _sgu_kernel(x_ref, pre_g_ref, post_g_ref, wu_ref, wv_ref, lng_ref, lnb_ref, ws_ref, bs_ref,
                wout_ref, o_ref, xn_ref, u_ref, v_ref):
    ts = x_ref.shape[0]
    nc, _, tc = v_ref.shape
    n_heads = ws_ref.shape[0]
    hd = (nc * tc) // n_heads
    j = pl.program_id(1)

    def project(xn):
        u_ref[j] = _gelu(_dot(xn, wu_ref[...]))
        v_ref[j] = _gelu(_dot(xn, wv_ref[...]))

    @pl.when(j == 0)
    def _():
        xn = _pre_norm_bf16(x_ref, pre_g_ref)
        xn_ref[...] = xn
        project(xn)

    @pl.when(j > 0)
    def _():
        project(xn_ref[...])

    @pl.when(j == nc - 1)
    def _():
        mu, rstd = _layer_norm_stats(v_ref)
        acc = jnp.zeros((ts, wout_ref.shape[1]), _F32)
        for c in range(nc):
            cols = slice(c * tc, (c + 1) * tc)
            vn = ((v_ref[c] - mu) * rstd * lng_ref[:, cols] + lnb_ref[:, cols]).astype(_BF16)
            gated = []
            for r in range(ts // SG_CHUNK):
                rows = slice(r * SG_CHUNK, (r + 1) * SG_CHUNK)
                parts = []
                for hh in range(tc // hd):
                    head = c * (tc // hd) + hh
                    s = _dot(ws_ref[head], vn[rows, hh * hd:(hh + 1) * hd])
                    parts.append(s + bs_ref[:, head:head + 1])
                gated.append(jnp.concatenate(parts, axis=1) if len(parts) > 1 else parts[0])
            sv = jnp.concatenate(gated, axis=0) if len(gated) > 1 else gated[0]
            acc = acc + _dot((u_ref[c] * sv).astype(_BF16), wout_ref[cols, :])
        _residual_out(x_ref, acc, post_g_ref, o_ref)


def _sgu_mixer(x, pre_g, post_g, w_in, ln_g, ln_b, w_s, b_s_t, w_out, *, ts, tc):
    n, d = x.shape
    nc = d // tc
    vec = lambda i, j: (0, 0)
    return pl.pallas_call(
        _sgu_kernel,
        grid=(n // ts, nc),
        in_specs=[
            pl.BlockSpec((ts, d), lambda i, j: (i, 0)),
            pl.BlockSpec((1, d), vec),
            pl.BlockSpec((1, d), vec),
            pl.BlockSpec((d, tc), lambda i, j: (0, j)),
            pl.BlockSpec((d, tc), lambda i, j: (0, nc + j)),
            pl.BlockSpec((1, d), vec),
            pl.BlockSpec((1, d), vec),
            pl.BlockSpec(w_s.shape, lambda i, j: (0, 0, 0)),
            pl.BlockSpec(b_s_t.shape, vec),
            pl.BlockSpec((d, d), vec),
        ],
        out_specs=pl.BlockSpec((ts, d), lambda i, j: (i, 0)),
        out_shape=jax.ShapeDtypeStruct((n, d), _F32),
        scratch_shapes=[
            pltpu.VMEM((ts, d), _BF16),
            pltpu.VMEM((nc, ts, tc), _F32),
            pltpu.VMEM((nc, ts, tc), _F32),
        ],
        compiler_params=_mixer_params(),
        name="sgu_mixer",
    )(x, pre_g, post_g, w_in, w_in, ln_g, ln_b, w_s, b_s_t, w_out)


def _tile_rows(seq_len):
    return min(512, seq_len)


def _run_trunk(x, p):
    b, s, d = x.shape
    ts = _tile_rows(s)
    tc = min(512, d)
    x = x.reshape(b * s, d)
    depth = p["mix_pre_g"].shape[0]
    for i in range(depth):
        kind, j = i % N_MIXERS, i // N_MIXERS
        pre_g, post_g = p["mix_pre_g"][i][None], p["mix_post_g"][i][None]
        if kind == 0:
            x = _pool_mixer(x, pre_g, post_g, p["pool_w_in"][j], p["pool_w_grp"][j],
                            p["pool_scale"][j][None], p["pool_w_out"][j], seq_len=s, ts=ts)
        elif kind == 1:
            x = _sconv_mixer(x, pre_g, post_g, p["sc_w_in"][j], p["sc_conv_w"][j],
                             p["sc_w_out"][j], seq_len=s, ts=ts, tc=tc)
        elif kind == 2:
            x = _conformer_mixer(x, pre_g, post_g, p["cf_w_in"][j], p["cf_dw_w"][j],
                                 p["cf_dw_b"][j][None], p["cf_ln_g"][j][None],
                                 p["cf_ln_b"][j][None], p["cf_w_out"][j], seq_len=s, ts=ts, tc=tc)
        else:
            x = _sgu_mixer(x, pre_g, post_g, p["sg_w_in"][j], p["sg_ln_g"][j][None],
                           p["sg_ln_b"][j][None], p["sg_w_s"][j], p["sg_b_s"][j].T,
                           p["sg_w_out"][j], ts=ts, tc=tc)
        x = _ffn(x, p["ffn_pre_g"][i][None], p["ffn_post_g"][i][None], p["ffn_w_gate"],
                 p["ffn_w_up"], p["ffn_w_down"], i, tm=ts)
    return x.reshape(b, s, d)


_MATMUL_WEIGHTS = ("ffn_w_gate", "ffn_w_up", "ffn_w_down", "pool_w_in", "pool_w_grp", "pool_w_out",
                   "sc_w_in", "sc_w_out", "cf_w_in", "cf_w_out", "sg_w_in", "sg_w_s", "sg_w_out")


def kernel(x_prompt, x_sample, mix_pre_g, mix_post_g, ffn_pre_g, ffn_post_g, ffn_w_gate, ffn_w_up,
           ffn_w_down, pool_w_in, pool_w_grp, pool_scale, pool_w_out, sc_w_in, sc_conv_w, sc_w_out,
           cf_w_in, cf_dw_w, cf_dw_b, cf_ln_g, cf_ln_b, cf_w_out, sg_w_in, sg_ln_g, sg_ln_b, sg_w_s,
           sg_b_s, sg_w_out):
    p = dict(mix_pre_g=mix_pre_g, mix_post_g=mix_post_g, ffn_pre_g=ffn_pre_g, ffn_post_g=ffn_post_g,
             ffn_w_gate=ffn_w_gate, ffn_w_up=ffn_w_up, ffn_w_down=ffn_w_down, pool_w_in=pool_w_in,
             pool_w_grp=pool_w_grp, pool_scale=pool_scale, pool_w_out=pool_w_out, sc_w_in=sc_w_in,
             sc_conv_w=sc_conv_w, sc_w_out=sc_w_out, cf_w_in=cf_w_in, cf_dw_w=cf_dw_w,
             cf_dw_b=cf_dw_b, cf_ln_g=cf_ln_g, cf_ln_b=cf_ln_b, cf_w_out=cf_w_out, sg_w_in=sg_w_in,
             sg_ln_g=sg_ln_g, sg_ln_b=sg_ln_b, sg_w_s=sg_w_s, sg_b_s=sg_b_s, sg_w_out=sg_w_out)
    for name in _MATMUL_WEIGHTS:
        p[name] = p[name].astype(_BF16)
    return _run_trunk(x_prompt, p), _run_trunk(x_sample, p)
```

```python
import functools

import jax
import jax.numpy as jnp
from jax import lax
from jax.experimental import pallas as pl
from jax.experimental.pallas import tpu as pltpu

EPS = 1e-6
POOL_WINDOWS = (2, 4, 8, 16)
SC_WIDTH = 3
CF_WIDTH = 31
SG_CHUNK = 128
SG_HEADS = 8
N_MIXERS = 4

HALO = 16
LANES = 128
SUBLANES = 8
VMEM_LIMIT_BYTES = 60 * 1024 * 1024

_BF16 = jnp.bfloat16
_F32 = jnp.float32


def _rms_norm(x, g):
    return x * lax.rsqrt(jnp.mean(x * x, axis=-1, keepdims=True) + EPS) * g


def _dot(a, b):
    return jnp.dot(a, b, preferred_element_type=_F32)


def _residual_out(x_ref, acc, post_g_ref, o_ref):
    o_ref[...] = x_ref[...] + _rms_norm(acc, post_g_ref[...])


NORM_SLICES = 8


def _pre_norm_bf16(x_ref, g_ref):
    rows = x_ref.shape[0] // NORM_SLICES
    g = g_ref[...]
    return jnp.concatenate(
        [_rms_norm(x_ref[pl.ds(k * rows, rows), :], g).astype(_BF16) for k in range(NORM_SLICES)],
        axis=0)


def _fill_xn_ext(xp_ref, x_ref, xn_ref, g_ref, xe_ref, tiles_per_seq):
    t = pl.program_id(0) % tiles_per_seq
    g = g_ref[...]
    prev = jnp.where(t > 0, _rms_norm(xp_ref[...], g), 0.0).astype(_BF16)
    nxt = jnp.where(t < tiles_per_seq - 1, _rms_norm(xn_ref[...], g), 0.0).astype(_BF16)
    xe = jnp.concatenate([prev, _pre_norm_bf16(x_ref, g_ref), nxt], axis=0)
    xe_ref[...] = xe
    return xe


FFN_SUBCHAINS = 2
FFN_TILE_ROWS = 1024


def _ffn_kernel(x_ref, pre_g_ref, post_g_ref, wg_ref, wu_ref, wd_ref, o_ref, xn_ref):
    j = pl.program_id(1)

    def swiglu_chunk(xn):
        sub = wg_ref.shape[1] // FFN_SUBCHAINS
        down = None
        for s in range(FFN_SUBCHAINS):
            cols = slice(s * sub, (s + 1) * sub)
            gate = _dot(xn, wg_ref[:, cols])
            up = _dot(xn, wu_ref[:, cols])
            a = (gate * jax.nn.sigmoid(gate) * up).astype(_BF16)
            part = _dot(a, wd_ref[cols, :])
            down = part if down is None else down + part
        return down

    @pl.when(j == 0)
    def _():
        xn = _pre_norm_bf16(x_ref, pre_g_ref)
        xn_ref[...] = xn
        o_ref[...] = swiglu_chunk(xn)

    @pl.when(j > 0)
    def _():
        o_ref[...] += swiglu_chunk(xn_ref[...])

    @pl.when(j == pl.num_programs(1) - 1)
    def _():
        _residual_out(x_ref, o_ref[...], post_g_ref, o_ref)


def _ffn(x, pre_g, post_g, wg, wu, wd, layer, *, tm, tf):
    n, d = x.shape
    f = wg.shape[2]
    row = lambda i, j: (i, 0)
    vec = lambda i, j: (0, 0)
    return pl.pallas_call(
        _ffn_kernel,
        grid=(n // tm, f // tf),
        in_specs=[
            pl.BlockSpec((tm, d), row),
            pl.BlockSpec((1, d), vec),
            pl.BlockSpec((1, d), vec),
            pl.BlockSpec((None, d, tf), lambda i, j: (layer, 0, j)),
            pl.BlockSpec((None, d, tf), lambda i, j: (layer, 0, j)),
            pl.BlockSpec((None, tf, d), lambda i, j: (layer, j, 0)),
        ],
        out_specs=pl.BlockSpec((tm, d), row),
        out_shape=jax.ShapeDtypeStruct((n, d), _F32),
        scratch_shapes=[pltpu.VMEM((tm, d), _BF16)],
        compiler_params=pltpu.CompilerParams(
            dimension_semantics=("arbitrary", "arbitrary"), vmem_limit_bytes=VMEM_LIMIT_BYTES),
        name="ffn",
    )(x, pre_g, post_g, wg, wu, wd)


def _halo_specs(ts, d, n_rows):
    r = ts // HALO
    last = n_rows // HALO - 1
    return [
        pl.BlockSpec((HALO, d), lambda i, j: (jnp.maximum(i * r - 1, 0), 0)),
        pl.BlockSpec((ts, d), lambda i, j: (i, 0)),
        pl.BlockSpec((HALO, d), lambda i, j: (jnp.minimum((i + 1) * r, last), 0)),
    ]


def _mixer_params():
    return pltpu.CompilerParams(
        dimension_semantics=("arbitrary", "arbitrary"), vmem_limit_bytes=VMEM_LIMIT_BYTES)


def _pool_kernel(xp_ref, x_ref, xn_ref, pre_g_ref, post_g_ref, win_ref, wgrp_ref, scale_ref,
                 wout_ref, o_ref, xe_ref, h_ref, ua_ref, ub_ref, acc_ref, *, seq_len):
    ts = x_ref.shape[0]
    ext = ts + 2 * HALO
    tiles_per_seq = seq_len // ts
    j = pl.program_id(1)

    @pl.when((pl.program_id(0) == 0) & (j == 0))
    def _():
        tail = jnp.zeros((SUBLANES, ua_ref.shape[1]), _F32)
        ua_ref[ext - SUBLANES:, :] = tail
        ub_ref[ext - SUBLANES:, :] = tail

    n_lvl = ext - 2 * SUBLANES

    def level(src_ref, dst_ref, n):
        dst_ref[pl.ds(SUBLANES, n_lvl), :] = (src_ref[pl.ds(SUBLANES, n_lvl), :]
                                              + src_ref[pl.ds(SUBLANES + n, n_lvl), :])

    def window_total(w):
        level(h_ref, ua_ref, 1)
        if w == 2:
            return ua_ref[pl.ds(HALO - 1, ts), :]
        level(ua_ref, ub_ref, 2)
        if w == 4:
            return ub_ref[pl.ds(HALO - 2, ts), :]
        level(ub_ref, ua_ref, 4)
        if w == 8:
            return ua_ref[pl.ds(HALO - 4, ts), :]
        return ua_ref[pl.ds(HALO - 8, ts), :] + ua_ref[pl.ds(HALO, ts), :]

    def group(xe, w):
        h_ref[...] = _dot(xe, win_ref[...])
        half = w // 2
        pos = (pl.program_id(0) % tiles_per_seq) * ts + lax.broadcasted_iota(jnp.int32, (ts, 1), 0)
        count = (jnp.minimum(pos + half, seq_len) - jnp.maximum(pos - half, 0)).astype(_F32)
        pooled = window_total(w) / count - h_ref[pl.ds(HALO, ts), :]
        mixed = _dot(pooled.astype(_BF16), wgrp_ref[...]) * scale_ref[...]
        return _dot(mixed.astype(_BF16), wout_ref[...])

    @pl.when(j == 0)
    def _():
        xe = _fill_xn_ext(xp_ref, x_ref, xn_ref, pre_g_ref, xe_ref, tiles_per_seq)
        acc_ref[...] = group(xe, POOL_WINDOWS[0])

    for g in range(1, len(POOL_WINDOWS)):
        @pl.when(j == g)
        def _(g=g):
            acc_ref[...] += group(xe_ref[...], POOL_WINDOWS[g])

    @pl.when(j == pl.num_programs(1) - 1)
    def _():
        _residual_out(x_ref, acc_ref[...], post_g_ref, o_ref)


def _pool_mixer(x, pre_g, post_g, w_in, w_grp, scale, w_out, *, seq_len, ts):
    n, d = x.shape
    ng, gd = w_grp.shape[0], w_grp.shape[1]
    vec = lambda i, j: (0, 0)
    assert POOL_WINDOWS == (2, 4, 8, 16) and ng == len(POOL_WINDOWS)
    return pl.pallas_call(
        functools.partial(_pool_kernel, seq_len=seq_len),
        grid=(n // ts, ng),
        in_specs=_halo_specs(ts, d, n) + [
            pl.BlockSpec((1, d), vec),
            pl.BlockSpec((1, d), vec),
            pl.BlockSpec((d, gd), lambda i, j: (0, j)),
            pl.BlockSpec((None, gd, gd), lambda i, j: (j, 0, 0)),
            pl.BlockSpec((1, gd), lambda i, j: (0, j)),
            pl.BlockSpec((gd, d), lambda i, j: (j, 0)),
        ],
        out_specs=pl.BlockSpec((ts, d), lambda i, j: (i, 0)),
        out_shape=jax.ShapeDtypeStruct((n, d), _F32),
        scratch_shapes=[
            pltpu.VMEM((ts + 2 * HALO, d), _BF16),
            pltpu.VMEM((ts + 2 * HALO, gd), _F32),
            pltpu.VMEM((ts + 2 * HALO, gd), _F32),
            pltpu.VMEM((ts + 2 * HALO, gd), _F32),
            pltpu.VMEM((ts, d), _F32),
        ],
        compiler_params=_mixer_params(),
        name="pool_mixer",
    )(x, x, x, pre_g, post_g, w_in, w_grp, scale, w_out)


def _sconv_kernel(xp_ref, x_ref, xn_ref, pre_g_ref, post_g_ref, wb_ref, wc_ref, wh_ref, cw_ref,
                  wout_ref, o_ref, xe_ref, p_ref, acc_ref, *, seq_len):
    ts = x_ref.shape[0]
    j = pl.program_id(1)

    def chunk(xe):
        p_ref[...] = _dot(xe, wc_ref[...]) * _dot(xe, wh_ref[...])
        half = SC_WIDTH // 2
        y = cw_ref[0:1, :] * p_ref[pl.ds(HALO - half, ts), :]
        for k in range(1, SC_WIDTH):
            y = y + cw_ref[k:k + 1, :] * p_ref[pl.ds(HALO - half + k, ts), :]
        gb = _dot(xe[HALO:HALO + ts, :], wb_ref[...])
        return _dot((gb * y).astype(_BF16), wout_ref[...])

    @pl.when(j == 0)
    def _():
        xe = _fill_xn_ext(xp_ref, x_ref, xn_ref, pre_g_ref, xe_ref, seq_len // ts)
        acc_ref[...] = chunk(xe)

    @pl.when(j > 0)
    def _():
        acc_ref[...] += chunk(xe_ref[...])

    @pl.when(j == pl.num_programs(1) - 1)
    def _():
        _residual_out(x_ref, acc_ref[...], post_g_ref, o_ref)


def _sconv_mixer(x, pre_g, post_g, w_in, conv_w, w_out, *, seq_len, ts, tc):
    n, d = x.shape
    nc = d // tc
    vec = lambda i, j: (0, 0)
    return pl.pallas_call(
        functools.partial(_sconv_kernel, seq_len=seq_len),
        grid=(n // ts, nc),
        in_specs=_halo_specs(ts, d, n) + [
            pl.BlockSpec((1, d), vec),
            pl.BlockSpec((1, d), vec),
            pl.BlockSpec((d, tc), lambda i, j: (0, j)),
            pl.BlockSpec((d, tc), lambda i, j: (0, nc + j)),
            pl.BlockSpec((d, tc), lambda i, j: (0, 2 * nc + j)),
            pl.BlockSpec((SC_WIDTH, tc), lambda i, j: (0, j)),
            pl.BlockSpec((tc, d), lambda i, j: (j, 0)),
        ],
        out_specs=pl.BlockSpec((ts, d), lambda i, j: (i, 0)),
        out_shape=jax.ShapeDtypeStruct((n, d), _F32),
        scratch_shapes=[
            pltpu.VMEM((ts + 2 * HALO, d), _BF16),
            pltpu.VMEM((ts + 2 * HALO, tc), _F32),
            pltpu.VMEM((ts, d), _F32),
        ],
        compiler_params=_mixer_params(),
        name="sconv_mixer",
    )(x, x, x, pre_g, post_g, w_in, w_in, w_in, conv_w, w_out)


def _layer_norm_stats(y_ref):
    nc, _, tc = y_ref.shape
    inv_d = 1.0 / (nc * tc)
    mu = sum(jnp.sum(y_ref[c], axis=-1, keepdims=True) for c in range(nc)) * inv_d
    var = sum(jnp.sum(jnp.square(y_ref[c] - mu), axis=-1, keepdims=True) for c in range(nc)) * inv_d
    return mu, lax.rsqrt(var + EPS)


CONV_ROWS = 64


def _glu_proj(xe, wa_ref, wgate_ref, p_ref):
    p_ref[...] = _dot(xe, wa_ref[...]) * jax.nn.sigmoid(_dot(xe, wgate_ref[...]))


def _depthwise_conv(p_ref, w_ref, b_ref, c, sh_ref, y_ref, ts):
    tc = p_ref.shape[1]
    n_sh = sh_ref.shape[2]
    for l in range(tc // LANES):
        lanes = pl.ds(l * LANES, LANES)
        sh = sh_ref.at[l % 2]
        for r in range(1, SUBLANES):
            sh[r - 1] = p_ref[pl.ds(r, n_sh), lanes]
        for rb in range(ts // CONV_ROWS):
            acc = None
            for r in range(SUBLANES):
                taps = [(k, (HALO - CF_WIDTH // 2 + k) // SUBLANES) for k in range(CF_WIDTH)
                        if (HALO - CF_WIDTH // 2 + k) % SUBLANES == r]
                a_lo, a_hi = taps[0][1], taps[-1][1]
                rows = pl.ds(a_lo * SUBLANES + rb * CONV_ROWS, CONV_ROWS + (a_hi - a_lo) * SUBLANES)
                span = p_ref[rows, lanes] if r == 0 else sh[r - 1, rows, :]
                for k, a in taps:
                    off = (a - a_lo) * SUBLANES
                    term = w_ref[c, k:k + 1, lanes] * span[off:off + CONV_ROWS, :]
                    acc = term if acc is None else acc + term
            y_ref[c, pl.ds(rb * CONV_ROWS, CONV_ROWS), lanes] = acc + b_ref[c, :, lanes]


def _conformer_kernel(xp_ref, x_ref, xn_ref, pre_g_ref, post_g_ref, wa_ref, wgate_ref, dww_ref,
                      dwb_ref, lng_ref, lnb_ref, wout_ref, o_ref, xe_ref, p0_ref, p1_ref, sh_ref,
                      y_ref, *, seq_len):
    ts = x_ref.shape[0]
    nc, _, tc = y_ref.shape
    j = pl.program_id(1)
    p_refs = (p0_ref, p1_ref)

    @pl.when(j == 0)
    def _():
        xe = _fill_xn_ext(xp_ref, x_ref, xn_ref, pre_g_ref, xe_ref, seq_len // ts)
        _glu_proj(xe, wa_ref, wgate_ref, p0_ref)

    for parity in (0, 1):
        @pl.when((j > 0) & (j % 2 == parity))
        def _(parity=parity):
            _depthwise_conv(p_refs[1 - parity], dww_ref, dwb_ref, j - 1, sh_ref, y_ref, ts)
            _glu_proj(xe_ref[...], wa_ref, wgate_ref, p_refs[parity])

    @pl.when(j == nc - 1)
    def _():
        _depthwise_conv(p_refs[(nc - 1) % 2], dww_ref, dwb_ref, nc - 1, sh_ref, y_ref, ts)

    @pl.when(j >= nc - 1)
    def _():
        mu, rstd = _layer_norm_stats(y_ref)
        acc = jnp.zeros((ts, wout_ref.shape[1]), _F32)
        for c in range(nc):
            cols = slice(c * tc, (c + 1) * tc)
            z = (y_ref[c] - mu) * rstd * lng_ref[:, cols] + lnb_ref[:, cols]
            z = z * jax.nn.sigmoid(z)
            acc = acc + _dot(z.astype(_BF16), wout_ref[cols, :])
        _residual_out(x_ref, acc, post_g_ref, o_ref)


def _conformer_mixer(x, pre_g, post_g, w_in, dw_w, dw_b, ln_g, ln_b, w_out, *, seq_len, ts, tc):
    n, d = x.shape
    nc = d // tc
    ext = ts + 2 * HALO
    vec = lambda i, j: (0, 0)
    vec3 = lambda i, j: (0, 0, 0)
    dw_w = dw_w.reshape(CF_WIDTH, nc, tc).transpose(1, 0, 2)
    dw_b = dw_b.reshape(nc, 1, tc)
    return pl.pallas_call(
        functools.partial(_conformer_kernel, seq_len=seq_len),
        grid=(n // ts, nc),
        in_specs=_halo_specs(ts, d, n) + [
            pl.BlockSpec((1, d), vec),
            pl.BlockSpec((1, d), vec),
            pl.BlockSpec((d, tc), lambda i, j: (0, j)),
            pl.BlockSpec((d, tc), lambda i, j: (0, nc + j)),
            pl.BlockSpec((nc, CF_WIDTH, tc), vec3),
            pl.BlockSpec((nc, 1, tc), vec3),
            pl.BlockSpec((1, d), vec),
            pl.BlockSpec((1, d), vec),
            pl.BlockSpec((d, d), vec, pipeline_mode=pl.Buffered(1)),
        ],
        out_specs=pl.BlockSpec((ts, d), lambda i, j: (i, 0)),
        out_shape=jax.ShapeDtypeStruct((n, d), _F32),
        scratch_shapes=[
            pltpu.VMEM((ext, d), _BF16),
            pltpu.VMEM((ext, tc), _F32),
            pltpu.VMEM((ext, tc), _F32),
            pltpu.VMEM((2, SUBLANES - 1, ext - SUBLANES, LANES), _F32),
            pltpu.VMEM((nc, ts, tc), _F32),
        ],
        compiler_params=_mixer_params(),
        name="conformer_mixer",
    )(x, x, x, pre_g, post_g, w_in, w_in, dw_w, dw_b, ln_g, ln_b, w_out)


def _gelu(x):
    return 0.5 * x * (1.0 + lax.erf(x * (2.0 ** -0.5)))


def _sgu_kernel(x_ref, pre_g_ref, post_g_ref, wu_ref, wv_ref, lng_ref, lnb_ref, ws_ref, bs_ref,
                wout_ref, o_ref, xn_ref, u_ref, v_ref):
    ts = x_ref.shape[0]
    nc, _, tc = v_ref.shape
    n_heads = ws_ref.shape[0]
    hd = (nc * tc) // n_heads
    j = pl.program_id(1)

    def project(xn):
        u_ref[j] = _gelu(_dot(xn, wu_ref[...]))
        v_ref[j] = _gelu(_dot(xn, wv_ref[...]))

    @pl.when(j == 0)
    def _():
        xn = _pre_norm_bf16(x_ref, pre_g_ref)
        xn_ref[...] = xn
        project(xn)

    @pl.when(j > 0)
    def _():
        project(xn_ref[...])

    @pl.when(j == nc - 1)
    def _():
        mu, rstd = _layer_norm_stats(v_ref)
        acc = jnp.zeros((ts, wout_ref.shape[1]), _F32)
        for c in range(nc):
            cols = slice(c * tc, (c + 1) * tc)
            vn = ((v_ref[c] - mu) * rstd * lng_ref[:, cols] + lnb_ref[:, cols]).astype(_BF16)
            gated = []
            for r in range(ts // SG_CHUNK):
                rows = slice(r * SG_CHUNK, (r + 1) * SG_CHUNK)
                parts = []
                for hh in range(tc // hd):
                    head = c * (tc // hd) + hh
                    s = _dot(ws_ref[head], vn[rows, hh * hd:(hh + 1) * hd])
                    parts.append(s + bs_ref[:, head:head + 1])
                gated.append(jnp.concatenate(parts, axis=1) if len(parts) > 1 else parts[0])
            sv = jnp.concatenate(gated, axis=0) if len(gated) > 1 else gated[0]
            acc = acc + _dot((u_ref[c] * sv).astype(_BF16), wout_ref[cols, :])
        _residual_out(x_ref, acc, post_g_ref, o_ref)


def _sgu_mixer(x, pre_g, post_g, w_in, ln_g, ln_b, w_s, b_s_t, w_out, *, ts, tc):
    n, d = x.shape
    nc = d // tc
    vec = lambda i, j: (0, 0)
    return pl.pallas_call(
        _sgu_kernel,
        grid=(n // ts, nc),
        in_specs=[
            pl.BlockSpec((ts, d), lambda i, j: (i, 0)),
            pl.BlockSpec((1, d), vec),
            pl.BlockSpec((1, d), vec),
            pl.BlockSpec((d, tc), lambda i, j: (0, j)),
            pl.BlockSpec((d, tc), lambda i, j: (0, nc + j)),
            pl.BlockSpec((1, d), vec),
            pl.BlockSpec((1, d), vec),
            pl.BlockSpec(w_s.shape, lambda i, j: (0, 0, 0)),
            pl.BlockSpec(b_s_t.shape, vec),
            pl.BlockSpec((d, d), vec),
        ],
        out_specs=pl.BlockSpec((ts, d), lambda i, j: (i, 0)),
        out_shape=jax.ShapeDtypeStruct((n, d), _F32),
        scratch_shapes=[
            pltpu.VMEM((ts, d), _BF16),
            pltpu.VMEM((nc, ts, tc), _F32),
            pltpu.VMEM((nc, ts, tc), _F32),
        ],
        compiler_params=_mixer_params(),
        name="sgu_mixer",
    )(x, pre_g, post_g, w_in, w_in, ln_g, ln_b, w_s, b_s_t, w_out)


def _tile_rows(seq_len):
    return min(512, seq_len)


def _run_trunk(x, p):
    b, s, d = x.shape
    ts = _tile_rows(s)
    tc = min(512, d)
    x = x.reshape(b * s, d)
    depth = p["mix_pre_g"].shape[0]
    for i in range(depth):
        kind, j = i % N_MIXERS, i // N_MIXERS
        pre_g, post_g = p["mix_pre_g"][i][None], p["mix_post_g"][i][None]
        if kind == 0:
            x = _pool_mixer(x, pre_g, post_g, p["pool_w_in"][j], p["pool_w_grp"][j],
                            p["pool_scale"][j][None], p["pool_w_out"][j], seq_len=s, ts=ts)
        elif kind == 1:
            x = _sconv_mixer(x, pre_g, post_g, p["sc_w_in"][j], p["sc_conv_w"][j],
                             p["sc_w_out"][j], seq_len=s, ts=ts, tc=tc)
        elif kind == 2:
            x = _conformer_mixer(x, pre_g, post_g, p["cf_w_in"][j], p["cf_dw_w"][j],
                                 p["cf_dw_b"][j][None], p["cf_ln_g"][j][None],
                                 p["cf_ln_b"][j][None], p["cf_w_out"][j], seq_len=s, ts=ts, tc=tc)
        else:
            x = _sgu_mixer(x, pre_g, post_g, p["sg_w_in"][j], p["sg_ln_g"][j][None],
                           p["sg_ln_b"][j][None], p["sg_w_s"][j], p["sg_b_s"][j].T,
                           p["sg_w_out"][j], ts=ts, tc=tc)
        x = _ffn(x, p["ffn_pre_g"][i][None], p["ffn_post_g"][i][None], p["ffn_w_gate"],
                 p["ffn_w_up"], p["ffn_w_down"], i, tm=min(FFN_TILE_ROWS, b * s),
                 tf=min(512, p["ffn_w_gate"].shape[2]))
    return x.reshape(b, s, d)


_MATMUL_WEIGHTS = ("ffn_w_gate", "ffn_w_up", "ffn_w_down", "pool_w_in", "pool_w_grp", "pool_w_out",
                   "sc_w_in", "sc_w_out", "cf_w_in", "cf_w_out", "sg_w_in", "sg_w_s", "sg_w_out")


def kernel(x_prompt, x_sample, mix_pre_g, mix_post_g, ffn_pre_g, ffn_post_g, ffn_w_gate, ffn_w_up,
           ffn_w_down, pool_w_in, pool_w_grp, pool_scale, pool_w_out, sc_w_in, sc_conv_w, sc_w_out,
           cf_w_in, cf_dw_w, cf_dw_b, cf_ln_g, cf_ln_b, cf_w_out, sg_w_in, sg_ln_g, sg_ln_b, sg_w_s,
           sg_b_s, sg_w_out):
    p = dict(mix_pre_g=mix_pre_g, mix_post_g=mix_post_g, ffn_pre_g=ffn_pre_g, ffn_post_g=ffn_post_g,
             ffn_w_gate=ffn_w_gate, ffn_w_up=ffn_w_up, ffn_w_down=ffn_w_down, pool_w_in=pool_w_in,
             pool_w_grp=pool_w_grp, pool_scale=pool_scale, pool_w_out=pool_w_out, sc_w_in=sc_w_in,
             sc_conv_w=sc_conv_w, sc_w_out=sc_w_out, cf_w_in=cf_w_in, cf_dw_w=cf_dw_w,
             cf_dw_b=cf_dw_b, cf_ln_g=cf_ln_g, cf_ln_b=cf_ln_b, cf_w_out=cf_w_out, sg_w_in=sg_w_in,
             sg_ln_g=sg_ln_g, sg_ln_b=sg_ln_b, sg_w_s=sg_w_s, sg_b_s=sg_b_s, sg_w_out=sg_w_out)
    for name in _MATMUL_WEIGHTS:
        p[name] = p[name].astype(_BF16)
    return _run_trunk(x_prompt, p), _run_trunk(x_sample, p)
```

```python
import functools

import jax
import jax.numpy as jnp
from jax import lax
from jax.experimental import pallas as pl
from jax.experimental.pallas import tpu as pltpu

EPS = 1e-6
POOL_WINDOWS = (2, 4, 8, 16)
SC_WIDTH = 3
CF_WIDTH = 31
SG_CHUNK = 128
SG_HEADS = 8
N_MIXERS = 4

HALO = 16
LANES = 128
SUBLANES = 8
VMEM_LIMIT_BYTES = 60 * 1024 * 1024

_BF16 = jnp.bfloat16
_F32 = jnp.float32


def _rms_norm(x, g):
    return x * lax.rsqrt(jnp.mean(x * x, axis=-1, keepdims=True) + EPS) * g


def _dot(a, b):
    return jnp.dot(a, b, preferred_element_type=_F32)


def _residual_out(x_ref, acc, post_g_ref, o_ref):
    o_ref[...] = x_ref[...] + _rms_norm(acc, post_g_ref[...])


NORM_SLICES = 8


def _pre_norm_bf16(x_ref, g_ref):
    rows = x_ref.shape[0] // NORM_SLICES
    g = g_ref[...]
    return jnp.concatenate(
        [_rms_norm(x_ref[pl.ds(k * rows, rows), :], g).astype(_BF16) for k in range(NORM_SLICES)],
        axis=0)


def _fill_xn_ext(xp_ref, x_ref, xn_ref, g_ref, xe_ref, tiles_per_seq):
    t = pl.program_id(0) % tiles_per_seq
    g = g_ref[...]
    prev = jnp.where(t > 0, _rms_norm(xp_ref[...], g), 0.0).astype(_BF16)
    nxt = jnp.where(t < tiles_per_seq - 1, _rms_norm(xn_ref[...], g), 0.0).astype(_BF16)
    xe = jnp.concatenate([prev, _pre_norm_bf16(x_ref, g_ref), nxt], axis=0)
    xe_ref[...] = xe
    return xe


FFN_SUBCHAINS = 2
FFN_TILE_ROWS = 1024


def _ffn_kernel(x_ref, pre_g_ref, post_g_ref, wg_ref, wu_ref, wd_ref, o_ref, xn_ref):
    j = pl.program_id(1)

    def swiglu_chunk(xn):
        sub = wg_ref.shape[1] // FFN_SUBCHAINS
        down = None
        for s in range(FFN_SUBCHAINS):
            cols = slice(s * sub, (s + 1) * sub)
            gate = _dot(xn, wg_ref[:, cols])
            up = _dot(xn, wu_ref[:, cols])
            a = (gate * jax.nn.sigmoid(gate) * up).astype(_BF16)
            part = _dot(a, wd_ref[cols, :])
            down = part if down is None else down + part
        return down

    @pl.when(j == 0)
    def _():
        xn = _pre_norm_bf16(x_ref, pre_g_ref)
        xn_ref[...] = xn
        o_ref[...] = swiglu_chunk(xn)

    @pl.when(j > 0)
    def _():
        o_ref[...] += swiglu_chunk(xn_ref[...])

    @pl.when(j == pl.num_programs(1) - 1)
    def _():
        _residual_out(x_ref, o_ref[...], post_g_ref, o_ref)


def _ffn(x, pre_g, post_g, wg, wu, wd, layer, *, tm, tf):
    n, d = x.shape
    f = wg.shape[2]
    row = lambda i, j: (i, 0)
    vec = lambda i, j: (0, 0)
    return pl.pallas_call(
        _ffn_kernel,
        grid=(n // tm, f // tf),
        in_specs=[
            pl.BlockSpec((tm, d), row),
            pl.BlockSpec((1, d), vec),
            pl.BlockSpec((1, d), vec),
            pl.BlockSpec((None, d, tf), lambda i, j: (layer, 0, j)),
            pl.BlockSpec((None, d, tf), lambda i, j: (layer, 0, j)),
            pl.BlockSpec((None, tf, d), lambda i, j: (layer, j, 0)),
        ],
        out_specs=pl.BlockSpec((tm, d), row),
        out_shape=jax.ShapeDtypeStruct((n, d), _F32),
        scratch_shapes=[pltpu.VMEM((tm, d), _BF16)],
        compiler_params=pltpu.CompilerParams(
            dimension_semantics=("arbitrary", "arbitrary"), vmem_limit_bytes=VMEM_LIMIT_BYTES),
        name="ffn",
    )(x, pre_g, post_g, wg, wu, wd)


def _halo_specs(ts, d, n_rows):
    r = ts // HALO
    last = n_rows // HALO - 1
    return [
        pl.BlockSpec((HALO, d), lambda i, j: (jnp.maximum(i * r - 1, 0), 0)),
        pl.BlockSpec((ts, d), lambda i, j: (i, 0)),
        pl.BlockSpec((HALO, d), lambda i, j: (jnp.minimum((i + 1) * r, last), 0)),
    ]


def _mixer_params():
    return pltpu.CompilerParams(
        dimension_semantics=("arbitrary", "arbitrary"), vmem_limit_bytes=VMEM_LIMIT_BYTES)


def _pool_kernel(xp_ref, x_ref, xn_ref, pre_g_ref, post_g_ref, win_ref, wgrp_ref, scale_ref,
                 wout_ref, o_ref, xe_ref, h_ref, ua_ref, ub_ref, *, seq_len):
    ts = x_ref.shape[0]
    ext = ts + 2 * HALO
    tiles_per_seq = seq_len // ts
    j = pl.program_id(1)

    @pl.when((pl.program_id(0) == 0) & (j == 0))
    def _():
        tail = jnp.zeros((SUBLANES, ua_ref.shape[1]), _F32)
        ua_ref[ext - SUBLANES:, :] = tail
        ub_ref[ext - SUBLANES:, :] = tail

    n_lvl = ext - 2 * SUBLANES

    def level(src_ref, dst_ref, n):
        dst_ref[pl.ds(SUBLANES, n_lvl), :] = (src_ref[pl.ds(SUBLANES, n_lvl), :]
                                              + src_ref[pl.ds(SUBLANES + n, n_lvl), :])

    def window_total(w):
        level(h_ref, ua_ref, 1)
        if w == 2:
            return ua_ref[pl.ds(HALO - 1, ts), :]
        level(ua_ref, ub_ref, 2)
        if w == 4:
            return ub_ref[pl.ds(HALO - 2, ts), :]
        level(ub_ref, ua_ref, 4)
        if w == 8:
            return ua_ref[pl.ds(HALO - 4, ts), :]
        return ua_ref[pl.ds(HALO - 8, ts), :] + ua_ref[pl.ds(HALO, ts), :]

    def group(xe, w):
        h_ref[...] = _dot(xe, win_ref[...])
        half = w // 2
        pos = (pl.program_id(0) % tiles_per_seq) * ts + lax.broadcasted_iota(jnp.int32, (ts, 1), 0)
        count = (jnp.minimum(pos + half, seq_len) - jnp.maximum(pos - half, 0)).astype(_F32)
        pooled = window_total(w) / count - h_ref[pl.ds(HALO, ts), :]
        mixed = _dot(pooled.astype(_BF16), wgrp_ref[...]) * scale_ref[...]
        return _dot(mixed.astype(_BF16), wout_ref[...])

    @pl.when(j == 0)
    def _():
        xe = _fill_xn_ext(xp_ref, x_ref, xn_ref, pre_g_ref, xe_ref, tiles_per_seq)
        o_ref[...] = group(xe, POOL_WINDOWS[0])

    for g in range(1, len(POOL_WINDOWS)):
        @pl.when(j == g)
        def _(g=g):
            o_ref[...] += group(xe_ref[...], POOL_WINDOWS[g])

    @pl.when(j == pl.num_programs(1) - 1)
    def _():
        _residual_out(x_ref, o_ref[...], post_g_ref, o_ref)


def _pool_mixer(x, pre_g, post_g, w_in, w_grp, scale, w_out, *, seq_len, ts):
    n, d = x.shape
    ng, gd = w_grp.shape[0], w_grp.shape[1]
    vec = lambda i, j: (0, 0)
    assert POOL_WINDOWS == (2, 4, 8, 16) and ng == len(POOL_WINDOWS)
    return pl.pallas_call(
        functools.partial(_pool_kernel, seq_len=seq_len),
        grid=(n // ts, ng),
        in_specs=_halo_specs(ts, d, n) + [
            pl.BlockSpec((1, d), vec),
            pl.BlockSpec((1, d), vec),
            pl.BlockSpec((d, gd), lambda i, j: (0, j)),
            pl.BlockSpec((None, gd, gd), lambda i, j: (j, 0, 0)),
            pl.BlockSpec((1, gd), lambda i, j: (0, j)),
            pl.BlockSpec((gd, d), lambda i, j: (j, 0)),
        ],
        out_specs=pl.BlockSpec((ts, d), lambda i, j: (i, 0)),
        out_shape=jax.ShapeDtypeStruct((n, d), _F32),
        scratch_shapes=[
            pltpu.VMEM((ts + 2 * HALO, d), _BF16),
            pltpu.VMEM((ts + 2 * HALO, gd), _F32),
            pltpu.VMEM((ts + 2 * HALO, gd), _F32),
            pltpu.VMEM((ts + 2 * HALO, gd), _F32),
        ],
        compiler_params=_mixer_params(),
        name="pool_mixer",
    )(x, x, x, pre_g, post_g, w_in, w_grp, scale, w_out)


def _sconv_kernel(xp_ref, x_ref, xn_ref, pre_g_ref, post_g_ref, wb_ref, wc_ref, wh_ref, cw_ref,
                  wout_ref, o_ref, xe_ref, p_ref, acc_ref, *, seq_len):
    ts = x_ref.shape[0]
    j = pl.program_id(1)

    def chunk(xe):
        p_ref[...] = _dot(xe, wc_ref[...]) * _dot(xe, wh_ref[...])
        half = SC_WIDTH // 2
        y = cw_ref[0:1, :] * p_ref[pl.ds(HALO - half, ts), :]
        for k in range(1, SC_WIDTH):
            y = y + cw_ref[k:k + 1, :] * p_ref[pl.ds(HALO - half + k, ts), :]
        gb = _dot(xe[HALO:HALO + ts, :], wb_ref[...])
        return _dot((gb * y).astype(_BF16), wout_ref[...])

    @pl.when(j == 0)
    def _():
        xe = _fill_xn_ext(xp_ref, x_ref, xn_ref, pre_g_ref, xe_ref, seq_len // ts)
        acc_ref[...] = chunk(xe)

    @pl.when(j > 0)
    def _():
        acc_ref[...] += chunk(xe_ref[...])

    @pl.when(j == pl.num_programs(1) - 1)
    def _():
        _residual_out(x_ref, acc_ref[...], post_g_ref, o_ref)


def _sconv_mixer(x, pre_g, post_g, w_in, conv_w, w_out, *, seq_len, ts, tc):
    n, d = x.shape
    nc = d // tc
    vec = lambda i, j: (0, 0)
    return pl.pallas_call(
        functools.partial(_sconv_kernel, seq_len=seq_len),
        grid=(n // ts, nc),
        in_specs=_halo_specs(ts, d, n) + [
            pl.BlockSpec((1, d), vec),
            pl.BlockSpec((1, d), vec),
            pl.BlockSpec((d, tc), lambda i, j: (0, j)),
            pl.BlockSpec((d, tc), lambda i, j: (0, nc + j)),
            pl.BlockSpec((d, tc), lambda i, j: (0, 2 * nc + j)),
            pl.BlockSpec((SC_WIDTH, tc), lambda i, j: (0, j)),
            pl.BlockSpec((tc, d), lambda i, j: (j, 0)),
        ],
        out_specs=pl.BlockSpec((ts, d), lambda i, j: (i, 0)),
        out_shape=jax.ShapeDtypeStruct((n, d), _F32),
        scratch_shapes=[
            pltpu.VMEM((ts + 2 * HALO, d), _BF16),
            pltpu.VMEM((ts + 2 * HALO, tc), _F32),
            pltpu.VMEM((ts, d), _F32),
        ],
        compiler_params=_mixer_params(),
        name="sconv_mixer",
    )(x, x, x, pre_g, post_g, w_in, w_in, w_in, conv_w, w_out)


def _layer_norm_stats(y_ref):
    nc, _, tc = y_ref.shape
    inv_d = 1.0 / (nc * tc)
    mu = sum(jnp.sum(y_ref[c], axis=-1, keepdims=True) for c in range(nc)) * inv_d
    var = sum(jnp.sum(jnp.square(y_ref[c] - mu), axis=-1, keepdims=True) for c in range(nc)) * inv_d
    return mu, lax.rsqrt(var + EPS)


CONV_ROWS = 64


def _glu_proj(xe, wa_ref, wgate_ref, p_ref):
    p_ref[...] = _dot(xe, wa_ref[...]) * jax.nn.sigmoid(_dot(xe, wgate_ref[...]))


def _depthwise_conv(p_ref, w_ref, b_ref, c, sh_ref, y_ref, ts):
    tc = p_ref.shape[1]
    n_sh = sh_ref.shape[2]
    for l in range(tc // LANES):
        lanes = pl.ds(l * LANES, LANES)
        sh = sh_ref.at[l % 2]
        for r in range(1, SUBLANES):
            sh[r - 1] = p_ref[pl.ds(r, n_sh), lanes]
        for rb in range(ts // CONV_ROWS):
            acc = None
            for r in range(SUBLANES):
                taps = [(k, (HALO - CF_WIDTH // 2 + k) // SUBLANES) for k in range(CF_WIDTH)
                        if (HALO - CF_WIDTH // 2 + k) % SUBLANES == r]
                a_lo, a_hi = taps[0][1], taps[-1][1]
                rows = pl.ds(a_lo * SUBLANES + rb * CONV_ROWS, CONV_ROWS + (a_hi - a_lo) * SUBLANES)
                span = p_ref[rows, lanes] if r == 0 else sh[r - 1, rows, :]
                for k, a in taps:
                    off = (a - a_lo) * SUBLANES
                    term = w_ref[c, k:k + 1, lanes] * span[off:off + CONV_ROWS, :]
                    acc = term if acc is None else acc + term
            y_ref[c, pl.ds(rb * CONV_ROWS, CONV_ROWS), lanes] = acc + b_ref[c, :, lanes]


def _conformer_kernel(xp_ref, x_ref, xn_ref, pre_g_ref, post_g_ref, wa_ref, wgate_ref, dww_ref,
                      dwb_ref, lng_ref, lnb_ref, wout_ref, o_ref, xe_ref, p0_ref, p1_ref, sh_ref,
                      y_ref, *, seq_len):
    ts = x_ref.shape[0]
    nc, _, tc = y_ref.shape
    j = pl.program_id(1)
    p_refs = (p0_ref, p1_ref)

    @pl.when(j == 0)
    def _():
        xe = _fill_xn_ext(xp_ref, x_ref, xn_ref, pre_g_ref, xe_ref, seq_len // ts)
        _glu_proj(xe, wa_ref, wgate_ref, p0_ref)

    for parity in (0, 1):
        @pl.when((j > 0) & (j % 2 == parity))
        def _(parity=parity):
            _depthwise_conv(p_refs[1 - parity], dww_ref, dwb_ref, j - 1, sh_ref, y_ref, ts)
            _glu_proj(xe_ref[...], wa_ref, wgate_ref, p_refs[parity])

    @pl.when(j == nc - 1)
    def _():
        _depthwise_conv(p_refs[(nc - 1) % 2], dww_ref, dwb_ref, nc - 1, sh_ref, y_ref, ts)

    @pl.when(j >= nc - 1)
    def _():
        mu, rstd = _layer_norm_stats(y_ref)
        acc = jnp.zeros((ts, wout_ref.shape[1]), _F32)
        for c in range(nc):
            cols = slice(c * tc, (c + 1) * tc)
            z = (y_ref[c] - mu) * rstd * lng_ref[:, cols] + lnb_ref[:, cols]
            z = z * jax.nn.sigmoid(z)
            acc = acc + _dot(z.astype(_BF16), wout_ref[cols, :])
        _residual_out(x_ref, acc, post_g_ref, o_ref)


def _conformer_mixer(x, pre_g, post_g, w_in, dw_w, dw_b, ln_g, ln_b, w_out, *, seq_len, ts, tc):
    n, d = x.shape
    nc = d // tc
    ext = ts + 2 * HALO
    vec = lambda i, j: (0, 0)
    vec3 = lambda i, j: (0, 0, 0)
    dw_w = dw_w.reshape(CF_WIDTH, nc, tc).transpose(1, 0, 2)
    dw_b = dw_b.reshape(nc, 1, tc)
    return pl.pallas_call(
        functools.partial(_conformer_kernel, seq_len=seq_len),
        grid=(n // ts, nc),
        in_specs=_halo_specs(ts, d, n) + [
            pl.BlockSpec((1, d), vec),
            pl.BlockSpec((1, d), vec),
            pl.BlockSpec((d, tc), lambda i, j: (0, j)),
            pl.BlockSpec((d, tc), lambda i, j: (0, nc + j)),
            pl.BlockSpec((nc, CF_WIDTH, tc), vec3),
            pl.BlockSpec((nc, 1, tc), vec3),
            pl.BlockSpec((1, d), vec),
            pl.BlockSpec((1, d), vec),
            pl.BlockSpec((d, d), vec, pipeline_mode=pl.Buffered(1)),
        ],
        out_specs=pl.BlockSpec((ts, d), lambda i, j: (i, 0)),
        out_shape=jax.ShapeDtypeStruct((n, d), _F32),
        scratch_shapes=[
            pltpu.VMEM((ext, d), _BF16),
            pltpu.VMEM((ext, tc), _F32),
            pltpu.VMEM((ext, tc), _F32),
            pltpu.VMEM((2, SUBLANES - 1, ext - SUBLANES, LANES), _F32),
            pltpu.VMEM((nc, ts, tc), _F32),
        ],
        compiler_params=_mixer_params(),
        name="conformer_mixer",
    )(x, x, x, pre_g, post_g, w_in, w_in, dw_w, dw_b, ln_g, ln_b, w_out)


def _gelu(x):
    return 0.5 * x * (1.0 + lax.erf(x * (2.0 ** -0.5)))


def _sgu_kernel(x_ref, pre_g_ref, post_g_ref, wu_ref, wv_ref, lng_ref, lnb_ref, ws_ref, bs_ref,
                wout_ref, o_ref, xn_ref, u_ref, v_ref):
    ts = x_ref.shape[0]
    nc, _, tc = v_ref.shape
    n_heads = ws_ref.shape[0]
    hd = (nc * tc) // n_heads
    j = pl.program_id(1)

    def project(xn):
        u_ref[j] = _gelu(_dot(xn, wu_ref[...]))
        v_ref[j] = _gelu(_dot(xn, wv_ref[...]))

    @pl.when(j == 0)
    def _():
        xn = _pre_norm_bf16(x_ref, pre_g_ref)
        xn_ref[...] = xn
        project(xn)

    @pl.when(j > 0)
    def _():
        project(xn_ref[...])

    @pl.when(j == nc - 1)
    def _():
        mu, rstd = _layer_norm_stats(v_ref)
        acc = jnp.zeros((ts, wout_ref.shape[1]), _F32)
        for c in range(nc):
            cols = slice(c * tc, (c + 1) * tc)
            vn = ((v_ref[c] - mu) * rstd * lng_ref[:, cols] + lnb_ref[:, cols]).astype(_BF16)
            gated = []
            for r in range(ts // SG_CHUNK):
                rows = slice(r * SG_CHUNK, (r + 1) * SG_CHUNK)
                parts = []
                for hh in range(tc // hd):
                    head = c * (tc // hd) + hh
                    s = _dot(ws_ref[head], vn[rows, hh * hd:(hh + 1) * hd])
                    parts.append(s + bs_ref[:, head:head + 1])
                gated.append(jnp.concatenate(parts, axis=1) if len(parts) > 1 else parts[0])
            sv = jnp.concatenate(gated, axis=0) if len(gated) > 1 else gated[0]
            acc = acc + _dot((u_ref[c] * sv).astype(_BF16), wout_ref[cols, :])
        _residual_out(x_ref, acc, post_g_ref, o_ref)


def _sgu_mixer(x, pre_g, post_g, w_in, ln_g, ln_b, w_s, b_s_t, w_out, *, ts, tc):
    n, d = x.shape
    nc = d // tc
    vec = lambda i, j: (0, 0)
    return pl.pallas_call(
        _sgu_kernel,
        grid=(n // ts, nc),
        in_specs=[
            pl.BlockSpec((ts, d), lambda i, j: (i, 0)),
            pl.BlockSpec((1, d), vec),
            pl.BlockSpec((1, d), vec),
            pl.BlockSpec((d, tc), lambda i, j: (0, j)),
            pl.BlockSpec((d, tc), lambda i, j: (0, nc + j)),
            pl.BlockSpec((1, d), vec),
            pl.BlockSpec((1, d), vec),
            pl.BlockSpec(w_s.shape, lambda i, j: (0, 0, 0)),
            pl.BlockSpec(b_s_t.shape, vec),
            pl.BlockSpec((d, d), vec),
        ],
        out_specs=pl.BlockSpec((ts, d), lambda i, j: (i, 0)),
        out_shape=jax.ShapeDtypeStruct((n, d), _F32),
        scratch_shapes=[
            pltpu.VMEM((ts, d), _BF16),
            pltpu.VMEM((nc, ts, tc), _F32),
            pltpu.VMEM((nc, ts, tc), _F32),
        ],
        compiler_params=_mixer_params(),
        name="sgu_mixer",
    )(x, pre_g, post_g, w_in, w_in, ln_g, ln_b, w_s, b_s_t, w_out)


def _tile_rows(seq_len):
    return min(512, seq_len)


def _run_trunk(x, p):
    b, s, d = x.shape
    ts = _tile_rows(s)
    tc = min(512, d)
    x = x.reshape(b * s, d)
    depth = p["mix_pre_g"].shape[0]
    for i in range(depth):
        kind, j = i % N_MIXERS, i // N_MIXERS
        pre_g, post_g = p["mix_pre_g"][i][None], p["mix_post_g"][i][None]
        if kind == 0:
            x = _pool_mixer(x, pre_g, post_g, p["pool_w_in"][j], p["pool_w_grp"][j],
                            p["pool_scale"][j][None], p["pool_w_out"][j], seq_len=s,
                            ts=min(FFN_TILE_ROWS, s))
        elif kind == 1:
            x = _sconv_mixer(x, pre_g, post_g, p["sc_w_in"][j], p["sc_conv_w"][j],
                             p["sc_w_out"][j], seq_len=s, ts=ts, tc=tc)
        elif kind == 2:
            x = _conformer_mixer(x, pre_g, post_g, p["cf_w_in"][j], p["cf_dw_w"][j],
                                 p["cf_dw_b"][j][None], p["cf_ln_g"][j][None],
                                 p["cf_ln_b"][j][None], p["cf_w_out"][j], seq_len=s, ts=ts, tc=tc)
        else:
            x = _sgu_mixer(x, pre_g, post_g, p["sg_w_in"][j], p["sg_ln_g"][j][None],
                           p["sg_ln_b"][j][None], p["sg_w_s"][j], p["sg_b_s"][j].T,
                           p["sg_w_out"][j], ts=ts, tc=tc)
        x = _ffn(x, p["ffn_pre_g"][i][None], p["ffn_post_g"][i][None], p["ffn_w_gate"],
                 p["ffn_w_up"], p["ffn_w_down"], i, tm=min(FFN_TILE_ROWS, b * s),
                 tf=min(512, p["ffn_w_gate"].shape[2]))
    return x.reshape(b, s, d)


_MATMUL_WEIGHTS = ("ffn_w_gate", "ffn_w_up", "ffn_w_down", "pool_w_in", "pool_w_grp", "pool_w_out",
                   "sc_w_in", "sc_w_out", "cf_w_in", "cf_w_out", "sg_w_in", "sg_w_s", "sg_w_out")


def kernel(x_prompt, x_sample, mix_pre_g, mix_post_g, ffn_pre_g, ffn_post_g, ffn_w_gate, ffn_w_up,
           ffn_w_down, pool_w_in, pool_w_grp, pool_scale, pool_w_out, sc_w_in, sc_conv_w, sc_w_out,
           cf_w_in, cf_dw_w, cf_dw_b, cf_ln_g, cf_ln_b, cf_w_out, sg_w_in, sg_ln_g, sg_ln_b, sg_w_s,
           sg_b_s, sg_w_out):
    p = dict(mix_pre_g=mix_pre_g, mix_post_g=mix_post_g, ffn_pre_g=ffn_pre_g, ffn_post_g=ffn_post_g,
             ffn_w_gate=ffn_w_gate, ffn_w_up=ffn_w_up, ffn_w_down=ffn_w_down, pool_w_in=pool_w_in,
             pool_w_grp=pool_w_grp, pool_scale=pool_scale, pool_w_out=pool_w_out, sc_w_in=sc_w_in,
             sc_conv_w=sc_conv_w, sc_w_out=sc_w_out, cf_w_in=cf_w_in, cf_dw_w=cf_dw_w,
             cf_dw_b=cf_dw_b, cf_ln_g=cf_ln_g, cf_ln_b=cf_ln_b, cf_w_out=cf_w_out, sg_w_in=sg_w_in,
             sg_ln_g=sg_ln_g, sg_ln_b=sg_ln_b, sg_w_s=sg_w_s, sg_b_s=sg_b_s, sg_w_out=sg_w_out)
    for name in _MATMUL_WEIGHTS:
        p[name] = p[name].astype(_BF16)
    return _run_trunk(x_prompt, p), _run_trunk(x_sample, p)
```
